```python
import math
import jax, jax.numpy as jnp
from jax import lax
import numpy as np

D_MODEL = 2048
BATCH = 4
SEQ = 2048
DEPTH = 4
DEC_BATCH = 8
DEC_SEQ = 4
PAST_LEN = 16384
PAGE_SIZE = 128

N_MIXERS = 2
N_HEADS = 16
HEAD_DIM = 64
V_DIM = 2 * HEAD_DIM
ATTN_WIDTH = N_HEADS * 2 * HEAD_DIM
ROT_DIM = HEAD_DIM // 4
ROPE_THETA = 500000.0
Q_BLOCK = 128
D_RNN = (4 * D_MODEL // 3) // 128 * 128
RG_HEADS = 16
RG_BLOCK = D_RNN // RG_HEADS
CONV_W = 4
RG_C = 8.0
D_FF = 5632
EPS = 1e-6

kernel_name = "macaron_diffattn_rglru_hybrid_step"


def rmsnorm(x, g):
    xf = x.astype(jnp.float32)
    y = xf * lax.rsqrt(jnp.mean(xf * xf, axis=-1, keepdims=True) + EPS)
    return (y * g.astype(jnp.float32)).astype(x.dtype)


def swiglu(x, w1, w3, w2):
    return (jax.nn.silu(x @ w1) * (x @ w3)) @ w2


def rope(x, pos):
    inv = ROPE_THETA ** (-jnp.arange(0, ROT_DIM, 2, dtype=jnp.float32) / ROT_DIM)
    ang = pos.astype(jnp.float32)[:, None] * inv[None, :]
    cos = jnp.cos(ang)[:, None, None, :]
    sin = jnp.sin(ang)[:, None, None, :]
    xr = x[..., :ROT_DIM].astype(jnp.float32)
    x1, x2 = xr[..., :ROT_DIM // 2], xr[..., ROT_DIM // 2:]
    rot = jnp.concatenate([x1 * cos - x2 * sin, x2 * cos + x1 * sin], axis=-1).astype(x.dtype)
    return jnp.concatenate([rot, x[..., ROT_DIM:]], axis=-1)


def diff_scores(q, k, mask):
    s = jnp.einsum('bthcd,bshcd->bhcts', q, k).astype(jnp.float32) * (HEAD_DIM ** -0.5)
    if mask is None:
        return s
    return jnp.where(mask, s, -jnp.inf)


def diff_combine(s_list, v_list, lam):
    p = jax.nn.softmax(jnp.concatenate(s_list, axis=-1), axis=-1)
    w = p[:, :, 0] - lam * p[:, :, 1]
    out = None
    start = 0
    for s_i, v_i in zip(s_list, v_list):
        n = s_i.shape[-1]
        o = jnp.einsum('bhts,bshe->bthe', w[..., start:start + n].astype(v_i.dtype), v_i)
        out = o if out is None else out + o
        start += n
    return out


def prompt_attention(q, k, v, lam):
    B, S = q.shape[0], q.shape[1]
    nb = S // Q_BLOCK
    qb = q.reshape(B, nb, Q_BLOCK, N_HEADS, 2, HEAD_DIM).swapaxes(0, 1)
    kpos = jnp.arange(S)

    def block(args):
        qblk, i = args
        qpos = i * Q_BLOCK + jnp.arange(Q_BLOCK)
        mask = kpos[None, :] <= qpos[:, None]
        return diff_combine([diff_scores(qblk, k, mask)], [v], lam)

    o = lax.map(block, (qb, jnp.arange(nb)))
    return o.swapaxes(0, 1).reshape(B, S, N_HEADS, V_DIM)


def sample_attention(q, k_new, v_new, k_past, v_past, lam):
    T = q.shape[1]
    s_past = diff_scores(q, k_past, None)
    s_new = diff_scores(q, k_new, jnp.tril(jnp.ones((T, T), dtype=bool)))
    return diff_combine([s_past, s_new], [v_past, v_new], lam)


def gather_pages(cache_j_pool, j, page_table):
    g = cache_j_pool[j, page_table]
    db, n_pages, ps = g.shape[0], g.shape[1], g.shape[2]
    return g.reshape(db, n_pages * ps, g.shape[3], g.shape[4])


def causal_conv(u, buf, w, b):
    T = u.shape[1]
    up = jnp.concatenate([buf.astype(u.dtype), u], axis=1)
    out = b + up[:, 0:T] * w[0]
    for j in range(1, CONV_W):
        out = out + up[:, j:j + T] * w[j]
    return out, up[:, -(CONV_W - 1):]


def block_diag(u, w, b):
    B, T = u.shape[0], u.shape[1]
    ub = u.reshape(B, T, RG_HEADS, RG_BLOCK)
    return jnp.einsum('bthi,hij->bthj', ub, w).reshape(B, T, D_RNN) + b


def rg_lru(u, h0, wa, ba, wx, bx, lam):
    r = jax.nn.sigmoid(block_diag(u, wa, ba).astype(jnp.float32))
    i = jax.nn.sigmoid(block_diag(u, wx, bx).astype(jnp.float32))
    log_a = -RG_C * r * jax.nn.softplus(-lam.astype(jnp.float32))
    a = jnp.exp(log_a)
    xin = jnp.sqrt(-jnp.expm1(2.0 * log_a)) * i * u.astype(jnp.float32)

    def step(h, inp):
        a_t, x_t = inp
        h = a_t * h + x_t
        return h, h

    hT, hs = lax.scan(step, h0.astype(jnp.float32), (a.swapaxes(0, 1), xin.swapaxes(0, 1)))
    return hs.swapaxes(0, 1).astype(u.dtype), hT


def run_trunk(x, pos, p, attn_mix, rec_h0, rec_buf0):
    B, T = x.shape[0], x.shape[1]
    new_k, new_v, new_h, new_buf = [], [], [], []
    for l in range(DEPTH):
        j = l // N_MIXERS
        x = x + 0.5 * swiglu(rmsnorm(x, p['ln_ffn_pre'][l]), p['ffn_pre_w1'][l],
                             p['ffn_pre_w3'][l], p['ffn_pre_w2'][l])
        hn = rmsnorm(x, p['ln_mix'][l])
        if l % N_MIXERS == 0:
            lam_init = 0.8 - 0.6 * math.exp(-0.3 * l)
            q = rope((hn @ p['attn_wq'][j]).reshape(B, T, N_HEADS, 2, HEAD_DIM), pos)
            k = rope((hn @ p['attn_wk'][j]).reshape(B, T, N_HEADS, 2, HEAD_DIM), pos)
            v = (hn @ p['attn_wv'][j]).reshape(B, T, N_HEADS, V_DIM)
            lam = (jnp.exp(jnp.sum(p['attn_lq1'][j].astype(jnp.float32) * p['attn_lk1'][j].astype(jnp.float32)))
                   - jnp.exp(jnp.sum(p['attn_lq2'][j].astype(jnp.float32) * p['attn_lk2'][j].astype(jnp.float32)))
                   + lam_init)
            o = attn_mix(j, q, k, v, lam)
            o = rmsnorm(o, p['attn_subln'][j]) * (1.0 - lam_init)
            x = x + o.reshape(B, T, ATTN_WIDTH).astype(x.dtype) @ p['attn_wo'][j]
            new_k.append(k.reshape(B, T, N_HEADS, V_DIM))
            new_v.append(v)
        else:
            gate = jax.nn.gelu(hn @ p['rec_w_gate'][j] + p['rec_b_gate'][j])
            u = hn @ p['rec_w_in'][j] + p['rec_b_in'][j]
            u, buf = causal_conv(u, rec_buf0[j], p['rec_conv_w'][j], p['rec_conv_b'][j])
            y, hT = rg_lru(u, rec_h0[j], p['rec_wa'][j], p['rec_ba'][j],
                           p['rec_wx'][j], p['rec_bx'][j], p['rec_lam'][j])
            x = x + (y * gate) @ p['rec_w_out'][j]
            new_h.append(hT)
            new_buf.append(buf)
        x = x + 0.5 * swiglu(rmsnorm(x, p['ln_ffn_post'][l]), p['ffn_post_w1'][l],
                             p['ffn_post_w3'][l], p['ffn_post_w2'][l])
    y = rmsnorm(x, p['ln_final'])
    return y, jnp.stack(new_k), jnp.stack(new_v), jnp.stack(new_h), jnp.stack(new_buf)


def setup_inputs(seed: int = 0) -> dict:
    key = jax.random.key(seed)
    ks = iter(jax.random.split(key, 64))
    f32 = jnp.float32
    n_a = (DEPTH + N_MIXERS - 1) // N_MIXERS
    n_b = DEPTH // N_MIXERS
    n_pages = PAST_LEN // PAGE_SIZE
    pool = (DEC_BATCH * n_pages * 5) // 4

    def nrm(shape, scale):
        return jax.random.normal(next(ks), shape, f32) * scale

    def gain(shape):
        return 1.0 + nrm(shape, 0.02)

    u = jax.random.uniform(next(ks), (n_b, D_RNN), f32, 0.9, 0.999)
    a0 = u ** (1.0 / RG_C)
    rec_lam = jnp.log(a0) - jnp.log1p(-a0)
    page_table = jax.random.permutation(next(ks), pool)[:DEC_BATCH * n_pages]
    page_table = page_table.reshape(DEC_BATCH, n_pages).astype(jnp.int32)

    return {
        'x_prompt': nrm((BATCH, SEQ, D_MODEL), 1.0),
        'x_sample': nrm((DEC_BATCH, DEC_SEQ, D_MODEL), 1.0),
        'cache_k': nrm((n_a, pool, PAGE_SIZE, N_HEADS, V_DIM), 1.0),
        'cache_v': nrm((n_a, pool, PAGE_SIZE, N_HEADS, V_DIM), 1.0),
        'state_h': nrm((n_b, DEC_BATCH, D_RNN), 0.5),
        'state_conv': nrm((n_b, DEC_BATCH, CONV_W - 1, D_RNN), 1.0),
        'page_table': page_table,
        'ln_ffn_pre': gain((DEPTH, D_MODEL)),
        'ffn_pre_w1': nrm((DEPTH, D_MODEL, D_FF), D_MODEL ** -0.5),
        'ffn_pre_w3': nrm((DEPTH, D_MODEL, D_FF), D_MODEL ** -0.5),
        'ffn_pre_w2': nrm((DEPTH, D_FF, D_MODEL), D_FF ** -0.5),
        'ln_mix': gain((DEPTH, D_MODEL)),
        'ln_ffn_post': gain((DEPTH, D_MODEL)),
        'ffn_post_w1': nrm((DEPTH, D_MODEL, D_FF), D_MODEL ** -0.5),
        'ffn_post_w3': nrm((DEPTH, D_MODEL, D_FF), D_MODEL ** -0.5),
        'ffn_post_w2': nrm((DEPTH, D_FF, D_MODEL), D_FF ** -0.5),
        'attn_wq': nrm((n_a, D_MODEL, ATTN_WIDTH), D_MODEL ** -0.5),
        'attn_wk': nrm((n_a, D_MODEL, ATTN_WIDTH), D_MODEL ** -0.5),
        'attn_wv': nrm((n_a, D_MODEL, ATTN_WIDTH), D_MODEL ** -0.5),
        'attn_lq1': nrm((n_a, HEAD_DIM), 0.1),
        'attn_lk1': nrm((n_a, HEAD_DIM), 0.1),
        'attn_lq2': nrm((n_a, HEAD_DIM), 0.1),
        'attn_lk2': nrm((n_a, HEAD_DIM), 0.1),
        'attn_subln': gain((n_a, V_DIM)),
        'attn_wo': nrm((n_a, ATTN_WIDTH, D_MODEL), ATTN_WIDTH ** -0.5),
        'rec_w_gate': nrm((n_b, D_MODEL, D_RNN), D_MODEL ** -0.5),
        'rec_b_gate': nrm((n_b, D_RNN), 0.02),
        'rec_w_in': nrm((n_b, D_MODEL, D_RNN), D_MODEL ** -0.5),
        'rec_b_in': nrm((n_b, D_RNN), 0.02),
        'rec_conv_w': nrm((n_b, CONV_W, D_RNN), CONV_W ** -0.5),
        'rec_conv_b': nrm((n_b, D_RNN), 0.02),
        'rec_wa': nrm((n_b, RG_HEADS, RG_BLOCK, RG_BLOCK), RG_BLOCK ** -0.5),
        'rec_ba': nrm((n_b, D_RNN), 0.02),
        'rec_wx': nrm((n_b, RG_HEADS, RG_BLOCK, RG_BLOCK), RG_BLOCK ** -0.5),
        'rec_bx': nrm((n_b, D_RNN), 0.02),
        'rec_lam': rec_lam,
        'rec_w_out': nrm((n_b, D_RNN, D_MODEL), D_RNN ** -0.5),
        'ln_final': gain((D_MODEL,)),
    }


def reference(x_prompt, x_sample, cache_k, cache_v, state_h, state_conv, page_table,
              ln_ffn_pre, ffn_pre_w1, ffn_pre_w3, ffn_pre_w2, ln_mix, ln_ffn_post,
              ffn_post_w1, ffn_post_w3, ffn_post_w2, attn_wq, attn_wk, attn_wv,
              attn_lq1, attn_lk1, attn_lq2, attn_lk2, attn_subln, attn_wo,
              rec_w_gate, rec_b_gate, rec_w_in, rec_b_in, rec_conv_w, rec_conv_b,
              rec_wa, rec_ba, rec_wx, rec_bx, rec_lam, rec_w_out, ln_final):
    p = {
        'ln_ffn_pre': ln_ffn_pre, 'ffn_pre_w1': ffn_pre_w1, 'ffn_pre_w3': ffn_pre_w3,
        'ffn_pre_w2': ffn_pre_w2, 'ln_mix': ln_mix, 'ln_ffn_post': ln_ffn_post,
        'ffn_post_w1': ffn_post_w1, 'ffn_post_w3': ffn_post_w3, 'ffn_post_w2': ffn_post_w2,
        'attn_wq': attn_wq, 'attn_wk': attn_wk, 'attn_wv': attn_wv,
        'attn_lq1': attn_lq1, 'attn_lk1': attn_lk1, 'attn_lq2': attn_lq2, 'attn_lk2': attn_lk2,
        'attn_subln': attn_subln, 'attn_wo': attn_wo,
        'rec_w_gate': rec_w_gate, 'rec_b_gate': rec_b_gate, 'rec_w_in': rec_w_in,
        'rec_b_in': rec_b_in, 'rec_conv_w': rec_conv_w, 'rec_conv_b': rec_conv_b,
        'rec_wa': rec_wa, 'rec_ba': rec_ba, 'rec_wx': rec_wx, 'rec_bx': rec_bx,
        'rec_lam': rec_lam, 'rec_w_out': rec_w_out, 'ln_final': ln_final,
    }
    n_b = rec_lam.shape[0]
    B, S = x_prompt.shape[0], x_prompt.shape[1]
    DB, T = x_sample.shape[0], x_sample.shape[1]
    past = page_table.shape[1] * cache_k.shape[2]

    def prompt_mix(j, q, k, v, lam):
        return prompt_attention(q, k, v, lam)

    h0_p = jnp.zeros((n_b, B, D_RNN), jnp.float32)
    buf0_p = jnp.zeros((n_b, B, CONV_W - 1, D_RNN), x_prompt.dtype)
    y_prompt, k_prompt, v_prompt, h_prompt, conv_prompt = run_trunk(
        x_prompt, jnp.arange(S), p, prompt_mix, h0_p, buf0_p)

    def sample_mix(j, q, k, v, lam):
        k_past = gather_pages(cache_k, j, page_table).reshape(DB, past, N_HEADS, 2, HEAD_DIM)
        v_past = gather_pages(cache_v, j, page_table)
        return sample_attention(q, k, v, k_past.astype(q.dtype), v_past.astype(v.dtype), lam)

    y_sample, k_sample, v_sample, h_sample, conv_sample = run_trunk(
        x_sample, past + jnp.arange(T), p, sample_mix, state_h, state_conv)

    return (y_prompt, y_sample, k_prompt, v_prompt, h_prompt, conv_prompt,
            k_sample, v_sample, h_sample, conv_sample)
```

```python
import functools
import math

import jax
import jax.numpy as jnp
from jax import lax
from jax.experimental import pallas as pl
from jax.experimental.pallas import tpu as pltpu

F32 = jnp.float32
BF16 = jnp.bfloat16

D_MODEL = 2048
DEPTH = 4
N_HEADS = 16
HEAD_DIM = 64
V_DIM = 2 * HEAD_DIM
ATTN_WIDTH = N_HEADS * V_DIM
ROT_DIM = HEAD_DIM // 4
ROPE_THETA = 500000.0
D_RNN = 2688
RG_HEADS = 16
RG_BLOCK = D_RNN // RG_HEADS
CONV_W = 4
RG_C = 8.0
D_FF = 5632
EPS = 1e-6

LANES = 128
SUBLANES = 8
VMEM_LIMIT_BYTES = 56 * 1024 * 1024
NEG_INF = float("-inf")


def _params(*sem):
    return pltpu.CompilerParams(dimension_semantics=sem, vmem_limit_bytes=VMEM_LIMIT_BYTES)


def _rms(x, g):
    ms = jnp.mean(x * x, axis=-1, keepdims=True)
    return x * lax.rsqrt(ms + EPS) * g


def _dot(a, b):
    return jnp.dot(a, b, preferred_element_type=F32)


def _dot_nt(a, b):
    return lax.dot_general(a, b, (((1,), (1,)), ((), ())), preferred_element_type=F32)


def _row_tile(m, cap):
    return cap if m % cap == 0 else m


def _ffn_body(x_ref, g_ref, w1_ref, w3_ref, w2_ref, gf_ref, o_ref, xn_ref, *, n_f, final_norm):
    j = pl.program_id(1)

    @pl.when(j == 0)
    def _():
        xn_ref[...] = _rms(x_ref[...], g_ref[...]).astype(BF16)
        o_ref[...] = jnp.zeros_like(o_ref)

    xn = xn_ref[...]
    a = _dot(xn, w1_ref[...])
    b = _dot(xn, w3_ref[...])
    h = (jax.nn.silu(a) * b).astype(BF16)
    o_ref[...] += _dot(h, w2_ref[...])

    @pl.when(j == n_f - 1)
    def _():
        y = x_ref[...] + 0.5 * o_ref[...]
        if final_norm:
            y = _rms(y, gf_ref[...])
        o_ref[...] = y


def _ffn(x, g, w1, w3, w2, layer, gf=None):
    m = x.shape[0]
    tm = _row_tile(m, 1024)
    tf = 512
    n_f = D_FF // tf
    final_norm = gf is not None
    if gf is None:
        gf = g
    body = functools.partial(_ffn_body, n_f=n_f, final_norm=final_norm)
    return pl.pallas_call(
        body,
        grid=(m // tm, n_f),
        in_specs=[
            pl.BlockSpec((tm, D_MODEL), lambda i, j: (i, 0), pipeline_mode=pl.Buffered(1)),
            pl.BlockSpec((1, D_MODEL), lambda i, j: (0, 0)),
            pl.BlockSpec((None, D_MODEL, tf), lambda i, j: (layer, 0, j)),
            pl.BlockSpec((None, D_MODEL, tf), lambda i, j: (layer, 0, j)),
            pl.BlockSpec((None, tf, D_MODEL), lambda i, j: (layer, j, 0)),
            pl.BlockSpec((1, D_MODEL), lambda i, j: (0, 0)),
        ],
        out_specs=pl.BlockSpec((tm, D_MODEL), lambda i, j: (i, 0)),
        out_shape=jax.ShapeDtypeStruct((m, D_MODEL), F32),
        scratch_shapes=[pltpu.VMEM((tm, D_MODEL), BF16)],
        compiler_params=_params("arbitrary", "arbitrary"),
        name="ffn",
    )(x, g, w1, w3, w2, gf)


def _mm_res_body(a_ref, w_ref, r_ref, o_ref, ab_ref):
    @pl.when(pl.program_id(1) == 0)
    def _():
        ab_ref[...] = a_ref[...].astype(BF16)

    o_ref[...] = r_ref[...] + _dot(ab_ref[...], w_ref[...])


def _mm_res(a, w, layer, resid):
    m, k = a.shape
    n = w.shape[-1]
    tm = _row_tile(m, 1024)
    tn = 1024
    return pl.pallas_call(
        _mm_res_body,
        grid=(m // tm, n // tn),
        in_specs=[
            pl.BlockSpec((tm, k), lambda i, j: (i, 0)),
            pl.BlockSpec((None, k, tn), lambda i, j: (layer, 0, j)),
            pl.BlockSpec((tm, tn), lambda i, j: (i, j)),
        ],
        out_specs=pl.BlockSpec((tm, tn), lambda i, j: (i, j)),
        out_shape=jax.ShapeDtypeStruct((m, n), F32),
        scratch_shapes=[pltpu.VMEM((tm, k), BF16)],
        compiler_params=_params("arbitrary", "arbitrary"),
        name="mm_res",
    )(a, w, resid)


def _qkv_body(x_ref, g_ref, w_ref, c_ref, sa_ref, sb_ref, q_ref, k_ref, v_ref, xn_ref, *, nq, tn):
    j = pl.program_id(1)

    @pl.when(j == 0)
    def _():
        xn_ref[...] = _rms(x_ref[...], g_ref[...]).astype(BF16)

    acc = _dot(xn_ref[...], w_ref[...])

    def rope(y):
        c, sa, sb = c_ref[...], sa_ref[...], sb_ref[...]
        outs = []
        for ch in range(tn // LANES):
            yc = y[:, ch * LANES:(ch + 1) * LANES]
            outs.append(yc * c + pltpu.roll(yc, LANES - ROT_DIM // 2, 1) * sa
                        + pltpu.roll(yc, ROT_DIM // 2, 1) * sb)
        return jnp.concatenate(outs, axis=1)

    @pl.when(j < nq)
    def _():
        q_ref[...] = (rope(acc) * (HEAD_DIM ** -0.5)).astype(BF16)

    @pl.when((j >= nq) & (j < 2 * nq))
    def _():
        k_ref[...] = rope(acc)

    @pl.when(j >= 2 * nq)
    def _():
        v_ref[...] = acc


def _qkv(x, g, wqkv, layer, tabs, period_rows):
    m = x.shape[0]
    tm = _row_tile(m, 512)
    tn = 1024
    nq = ATTN_WIDTH // tn
    pb = period_rows // tm
    body = functools.partial(_qkv_body, nq=nq, tn=tn)
    tab_spec = pl.BlockSpec((tm, LANES), lambda i, j: (i % pb, 0))
    return pl.pallas_call(
        body,
        grid=(m // tm, 3 * nq),
        in_specs=[
            pl.BlockSpec((tm, D_MODEL), lambda i, j: (i, 0)),
            pl.BlockSpec((1, D_MODEL), lambda i, j: (0, 0)),
            pl.BlockSpec((None, D_MODEL, tn), lambda i, j: (layer, 0, j)),
            tab_spec, tab_spec, tab_spec,
        ],
        out_specs=[
            pl.BlockSpec((tm, tn), lambda i, j: (i, jnp.minimum(j, nq - 1))),
            pl.BlockSpec((tm, tn), lambda i, j: (i, jnp.clip(j - nq, 0, nq - 1))),
            pl.BlockSpec((tm, tn), lambda i, j: (i, jnp.clip(j - 2 * nq, 0, nq - 1))),
        ],
        out_shape=[
            jax.ShapeDtypeStruct((m, ATTN_WIDTH), BF16),
            jax.ShapeDtypeStruct((m, ATTN_WIDTH), F32),
            jax.ShapeDtypeStruct((m, ATTN_WIDTH), F32),
        ],
        scratch_shapes=[pltpu.VMEM((tm, D_MODEL), BF16)],
        compiler_params=_params("arbitrary", "arbitrary"),
        name="qkv",
    )(x, g, wqkv, *tabs)


def _lam_from(lp):
    s1 = jnp.sum(lp[0:1, :] * lp[1:2, :], axis=-1, keepdims=True)
    s2 = jnp.sum(lp[2:3, :] * lp[3:4, :], axis=-1, keepdims=True)
    return jnp.exp(s1) - jnp.exp(s2)


def _pattn_body(lp_ref, g_ref, q_ref, k_ref, v_ref, o_ref, kb_ref, vb_ref, *, tq, lam_init):
    qi = pl.program_id(2)

    @pl.when(qi == 0)
    def _():
        kb_ref[...] = k_ref[...].astype(BF16)
        vb_ref[...] = v_ref[...].astype(BF16)

    q = q_ref[...]
    lane = lax.broadcasted_iota(jnp.int32, (tq, V_DIM), 1)
    zero = jnp.zeros_like(q)
    qs = jnp.concatenate([jnp.where(lane < HEAD_DIM, q, zero),
                          jnp.where(lane >= HEAD_DIM, q, zero)], axis=0)

    def scores(j):
        kt = kb_ref[pl.ds(pl.multiple_of(j * tq, tq), tq), :]
        return _dot_nt(qs, kt)

    def values(j):
        return vb_ref[pl.ds(pl.multiple_of(j * tq, tq), tq), :]

    row = lax.broadcasted_iota(jnp.int32, (2 * tq, tq), 0)
    col = lax.broadcasted_iota(jnp.int32, (2 * tq, tq), 1)
    qrow = jnp.where(row >= tq, row - tq, row)
    s = jnp.where(col <= qrow, scores(qi), NEG_INF)
    m0 = jnp.max(s, axis=-1, keepdims=True)
    p = jnp.exp(s - m0)
    l0 = jnp.sum(p, axis=-1, keepdims=True)
    acc0 = _dot(p.astype(BF16), values(qi))

    def body(j, carry):
        m, l, acc = carry
        s = scores(j)
        m_new = jnp.maximum(m, jnp.max(s, axis=-1, keepdims=True))
        alpha = jnp.exp(m - m_new)
        p = jnp.exp(s - m_new)
        l = alpha * l + jnp.sum(p, axis=-1, keepdims=True)
        acc = alpha * acc + _dot(p.astype(BF16), values(j))
        return m_new, l, acc

    _, l, acc = lax.fori_loop(0, qi, body, (m0, l0, acc0))
    o = acc / l
    lam = _lam_from(lp_ref[...]) + lam_init
    o = o[:tq] - lam * o[tq:]
    o_ref[...] = (_rms(o, g_ref[...]) * (1.0 - lam_init)).astype(BF16)


def _prompt_attention(q, k, v, lp, g, batch, seq, lam_init):
    tq = 256
    nq = seq // tq
    body = functools.partial(_pattn_body, tq=tq, lam_init=lam_init)
    return pl.pallas_call(
        body,
        grid=(batch, N_HEADS, nq),
        in_specs=[
            pl.BlockSpec((4, HEAD_DIM), lambda b, h, i: (0, 0)),
            pl.BlockSpec((1, V_DIM), lambda b, h, i: (0, 0)),
            pl.BlockSpec((tq, V_DIM), lambda b, h, i: (b * nq + i, h)),
            pl.BlockSpec((seq, V_DIM), lambda b, h, i: (b, h)),
            pl.BlockSpec((seq, V_DIM), lambda b, h, i: (b, h)),
        ],
        out_specs=pl.BlockSpec((tq, V_DIM), lambda b, h, i: (b * nq + i, h)),
        out_shape=jax.ShapeDtypeStruct((batch * seq, ATTN_WIDTH), BF16),
        scratch_shapes=[pltpu.VMEM((seq, V_DIM), BF16), pltpu.VMEM((seq, V_DIM), BF16)],
        compiler_params=_params("arbitrary", "arbitrary", "arbitrary"),
        name="prompt_attn",
    )(lp, g, q, k, v)


def _dattn_body(pt_ref, lp_ref, g_ref, q_ref, kn_ref, vn_ref, kp_ref, vp_ref, o_ref,
                m_ref, l_ref, acc_ref, bias_ref, *, n_pages, page, dec_seq, lam_init):
    del pt_ref
    p_id = pl.program_id(1)
    nq = 2 * dec_seq * N_HEADS
    q = q_ref[...]

    @pl.when(p_id == 0)
    def _():
        r = lax.broadcasted_iota(jnp.int32, (nq, page * N_HEADS), 0)
        c = lax.broadcasted_iota(jnp.int32, (nq, page * N_HEADS), 1)
        same_head = (r & (N_HEADS - 1)) == (c & (N_HEADS - 1))
        bias_ref[...] = jnp.where(same_head, 0.0, NEG_INF)
        nk = dec_seq * N_HEADS
        r2 = lax.broadcasted_iota(jnp.int32, (nq, nk), 0)
        c2 = lax.broadcasted_iota(jnp.int32, (nq, nk), 1)
        ok = ((r2 & (N_HEADS - 1)) == (c2 & (N_HEADS - 1))) & (lax.shift_right_logical(c2, 4) <= lax.shift_right_logical(r2, 5))
        s = jnp.where(ok, _dot_nt(q, kn_ref[...].astype(BF16)), NEG_INF)
        m = jnp.max(s, axis=-1, keepdims=True)
        pr = jnp.exp(s - m)
        m_ref[...] = m
        l_ref[...] = jnp.sum(pr, axis=-1, keepdims=True)
        acc_ref[...] = _dot(pr.astype(BF16), vn_ref[...].astype(BF16))

    k = kp_ref[...].reshape(page * N_HEADS, V_DIM).astype(BF16)
    v = vp_ref[...].reshape(page * N_HEADS, V_DIM).astype(BF16)
    s = _dot_nt(q, k) + bias_ref[...]
    m_old = m_ref[...]
    m_new = jnp.maximum(m_old, jnp.max(s, axis=-1, keepdims=True))
    alpha = jnp.exp(m_old - m_new)
    pr = jnp.exp(s - m_new)
    l_ref[...] = alpha * l_ref[...] + jnp.sum(pr, axis=-1, keepdims=True)
    acc_ref[...] = alpha * acc_ref[...] + _dot(pr.astype(BF16), v)
    m_ref[...] = m_new

    @pl.when(p_id == n_pages - 1)
    def _():
        o = acc_ref[...] / l_ref[...]
        lam = _lam_from(lp_ref[...]) + lam_init
        g = g_ref[...]
        for t in range(dec_seq):
            o1 = o[(2 * t) * N_HEADS:(2 * t + 1) * N_HEADS]
            o2 = o[(2 * t + 1) * N_HEADS:(2 * t + 2) * N_HEADS]
            o_ref[t] = _rms(o1 - lam * o2, g) * (1.0 - lam_init)


def _sample_attention(qall, k_new, v_new, cache_k, cache_v, slot, page_table, lp, g, lam_init):
    db, n_pages = page_table.shape
    page = cache_k.shape[2]
    dec_seq = k_new.shape[1] // N_HEADS
    nq = 2 * dec_seq * N_HEADS
    body = functools.partial(_dattn_body, n_pages=n_pages, page=page, dec_seq=dec_seq,
                             lam_init=lam_init)
    page_spec = pl.BlockSpec((None, None, page, N_HEADS, V_DIM),
                             lambda b, p, pt: (slot, pt[b * n_pages + p], 0, 0, 0))
    new_spec = pl.BlockSpec((None, dec_seq * N_HEADS, V_DIM), lambda b, p, pt: (b, 0, 0))
    grid_spec = pltpu.PrefetchScalarGridSpec(
        num_scalar_prefetch=1,
        grid=(db, n_pages),
        in_specs=[
            pl.BlockSpec((4, HEAD_DIM), lambda b, p, pt: (0, 0)),
            pl.BlockSpec((1, V_DIM), lambda b, p, pt: (0, 0)),
            pl.BlockSpec((None, nq, V_DIM), lambda b, p, pt: (b, 0, 0)),
            new_spec, new_spec, page_spec, page_spec,
        ],
        out_specs=pl.BlockSpec((None, dec_seq, N_HEADS, V_DIM), lambda b, p, pt: (b, 0, 0, 0)),
        scratch_shapes=[
            pltpu.VMEM((nq, 1), F32), pltpu.VMEM((nq, 1), F32), pltpu.VMEM((nq, V_DIM), F32),
            pltpu.VMEM((nq, page * N_HEADS), F32),
        ],
    )
    return pl.pallas_call(
        body,
        grid_spec=grid_spec,
        out_shape=jax.ShapeDtypeStruct((db, dec_seq, N_HEADS, V_DIM), F32),
        compiler_params=_params("arbitrary", "arbitrary"),
        name="sample_attn",
    )(page_table.reshape(-1), lp, g, qall, k_new, v_new, cache_k, cache_v)


def _recin_body(x_ref, g_ref, w_ref, b_ref, gate_ref, u_ref, xn_ref, *, nh):
    j = pl.program_id(1)

    @pl.when(j == 0)
    def _():
        xn_ref[...] = _rms(x_ref[...], g_ref[...]).astype(BF16)

    acc = _dot(xn_ref[...], w_ref[...]) + b_ref[...]

    @pl.when(j < nh)
    def _():
        gate_ref[...] = jax.nn.gelu(acc)

    @pl.when(j >= nh)
    def _():
        u_ref[...] = acc


def _rec_in(x, g, w_cat, b_cat, layer):
    m = x.shape[0]
    tm = _row_tile(m, 1024)
    tn = 896
    nh = D_RNN // tn
    body = functools.partial(_recin_body, nh=nh)
    return pl.pallas_call(
        body,
        grid=(m // tm, 2 * nh),
        in_specs=[
            pl.BlockSpec((tm, D_MODEL), lambda i, j: (i, 0)),
            pl.BlockSpec((1, D_MODEL), lambda i, j: (0, 0)),
            pl.BlockSpec((None, D_MODEL, tn), lambda i, j: (layer, 0, j)),
            pl.BlockSpec((None, 1, tn), lambda i, j: (layer, 0, j)),
        ],
        out_specs=[
            pl.BlockSpec((tm, tn), lambda i, j: (i, jnp.minimum(j, nh - 1))),
            pl.BlockSpec((tm, tn), lambda i, j: (i, jnp.maximum(j - nh, 0))),
        ],
        out_shape=[jax.ShapeDtypeStruct((m, D_RNN), F32), jax.ShapeDtypeStruct((m, D_RNN), F32)],
        scratch_shapes=[pltpu.VMEM((tm, D_MODEL), BF16)],
        compiler_params=_params("arbitrary", "arbitrary"),
        name="rec_in",
    )(x, g, w_cat, b_cat)


def _conv_body(u_ref, buf_ref, w_ref, b_ref, uc_ref, nb_ref, pad_ref, *, t_len):
    lo = SUBLANES - (CONV_W - 1)
    pad_ref[lo:SUBLANES, :] = buf_ref[...]
    pad_ref[SUBLANES:SUBLANES + t_len, :] = u_ref[...]
    out = b_ref[...] + pad_ref[lo:lo + t_len, :] * w_ref[0:1, :]
    for j in range(1, CONV_W):
        out = out + pad_ref[lo + j:lo + j + t_len, :] * w_ref[j:j + 1, :]
    uc_ref[...] = out
    nb_ref[...] = pad_ref[t_len + lo:t_len + SUBLANES, :]


def _conv(u, buf, w, b, layer):
    bsz, t_len, _ = u.shape
    tc = 384
    body = functools.partial(_conv_body, t_len=t_len)
    return pl.pallas_call(
        body,
        grid=(bsz, D_RNN // tc),
        in_specs=[
            pl.BlockSpec((None, t_len, tc), lambda b_, c: (b_, 0, c)),
            pl.BlockSpec((None, CONV_W - 1, tc), lambda b_, c: (b_, 0, c)),
            pl.BlockSpec((None, CONV_W, tc), lambda b_, c: (layer, 0, c)),
            pl.BlockSpec((None, 1, tc), lambda b_, c: (layer, 0, c)),
        ],
        out_specs=[
            pl.BlockSpec((None, t_len, tc), lambda b_, c: (b_, 0, c)),
            pl.BlockSpec((None, CONV_W - 1, tc), lambda b_, c: (b_, 0, c)),
        ],
        out_shape=[jax.ShapeDtypeStruct(u.shape, F32), jax.ShapeDtypeStruct(buf.shape, F32)],
        scratch_shapes=[pltpu.VMEM((t_len + SUBLANES, tc), F32)],
        compiler_params=_params("arbitrary", "arbitrary"),
        name="conv",
    )(u, buf, w, b)


def _softplus(z):
    return jnp.maximum(z, 0.0) + jnp.log1p(jnp.exp(-jnp.abs(z)))


def _gates_body(uc_ref, wa_ref, wx_ref, ba_ref, bx_ref, lam_ref, a_ref, xin_ref, ub_ref, *, tn):
    j = pl.program_id(1)

    @pl.when(j == 0)
    def _():
        ub_ref[...] = uc_ref[...].astype(BF16)

    ub = ub_ref[...]
    r = jax.nn.sigmoid(_dot(ub, wa_ref[...]) + ba_ref[...])
    i = jax.nn.sigmoid(_dot(ub, wx_ref[...]) + bx_ref[...])
    log_a = -RG_C * r * _softplus(-lam_ref[...])
    a = jnp.exp(log_a)
    u = uc_ref[:, pl.ds(pl.multiple_of(j * tn, LANES), tn)]
    a_ref[...] = a
    xin_ref[...] = jnp.sqrt(1.0 - a * a) * i * u


def _gates(uc, wa_d, wx_d, ba, bx, lam, layer):
    m = uc.shape[0]
    tm = _row_tile(m, 512)
    tn = 384
    body = functools.partial(_gates_body, tn=tn)
    vec = pl.BlockSpec((None, 1, tn), lambda i, j: (layer, 0, j))
    wsp = pl.BlockSpec((None, D_RNN, tn), lambda i, j: (layer, 0, j))
    osp = pl.BlockSpec((tm, tn), lambda i, j: (i, j))
    return pl.pallas_call(
        body,
        grid=(m // tm, D_RNN // tn),
        in_specs=[pl.BlockSpec((tm, D_RNN), lambda i, j: (i, 0)), wsp, wsp, vec, vec, vec],
        out_specs=[osp, osp],
        out_shape=[jax.ShapeDtypeStruct((m, D_RNN), F32), jax.ShapeDtypeStruct((m, D_RNN), F32)],
        scratch_shapes=[pltpu.VMEM((tm, D_RNN), BF16)],
        compiler_params=_params("arbitrary", "arbitrary"),
        name="gates",
    )(uc, wa_d, wx_d, ba, bx, lam)


def _scan_body(a_ref, x_ref, gate_ref, h0_ref, y_ref, ht_ref, *, t_len):
    tc = a_ref.shape[-1]
    h = h0_ref[...]
    if t_len % SUBLANES == 0:
        row = lax.broadcasted_iota(jnp.int32, (SUBLANES, tc), 0)

        def body(c, h):
            sl = pl.ds(pl.multiple_of(c * SUBLANES, SUBLANES), SUBLANES)
            a = a_ref[sl, :]
            x = x_ref[sl, :]
            d = 1
            while d < SUBLANES:
                valid = row >= d
                x = jnp.where(valid, a * pltpu.roll(x, d, 0) + x, x)
                a = jnp.where(valid, a * pltpu.roll(a, d, 0), a)
                d *= 2
            hs = a * h + x
            y_ref[sl, :] = hs * gate_ref[sl, :]
            return hs[SUBLANES - 1:SUBLANES, :]

        h = lax.fori_loop(0, t_len // SUBLANES, body, h)
    else:
        for t in range(t_len):
            h = a_ref[t:t + 1, :] * h + x_ref[t:t + 1, :]
            y_ref[t:t + 1, :] = h * gate_ref[t:t + 1, :]
    ht_ref[...] = h


def _scan(a, xin, gate, h0):
    bsz, t_len, _ = a.shape
    tc = 384
    body = functools.partial(_scan_body, t_len=t_len)
    seq_spec = pl.BlockSpec((None, t_len, tc), lambda b_, c: (b_, 0, c))
    h_spec = pl.BlockSpec((None, 1, tc), lambda b_, c: (b_, 0, c))
    return pl.pallas_call(
        body,
        grid=(bsz, D_RNN // tc),
        in_specs=[seq_spec, seq_spec, seq_spec, h_spec],
        out_specs=[seq_spec, h_spec],
        out_shape=[jax.ShapeDtypeStruct(a.shape, F32), jax.ShapeDtypeStruct((bsz, 1, D_RNN), F32)],
        compiler_params=_params("arbitrary", "arbitrary"),
        name="scan",
    )(a, xin, gate, h0)


def _rope_tables(pos):
    half = ROT_DIM // 2
    inv = ROPE_THETA ** (-jnp.arange(0, ROT_DIM, 2, dtype=F32) / ROT_DIM)
    ang = pos.astype(F32)[:, None] * inv[None, :]
    cos, sin = jnp.cos(ang), jnp.sin(ang)
    t_len = pos.shape[0]
    ones = jnp.ones((t_len, HEAD_DIM - ROT_DIM), F32)
    zer = jnp.zeros((t_len, HEAD_DIM - ROT_DIM), F32)
    zh = jnp.zeros((t_len, half), F32)
    c = jnp.concatenate([cos, cos, ones], axis=1)
    sa = jnp.concatenate([-sin, zh, zer], axis=1)
    sb = jnp.concatenate([zh, sin, zer], axis=1)
    return tuple(jnp.concatenate([t, t], axis=1) for t in (c, sa, sb))


def _block_diag_dense(w):
    n_l, n_h, bi, bj = w.shape
    eye = jnp.eye(n_h, dtype=w.dtype)
    return (w[:, :, :, None, :] * eye[None, :, None, :, None]).reshape(n_l, n_h * bi, n_h * bj)


def _trunk(x, bsz, t_len, p, tabs, period_rows, attn_fn, h0, buf0):
    ks, vs, hs, bufs = [], [], [], []
    for l in range(DEPTH):
        j = l // 2
        x = _ffn(x, p["ln_ffn_pre"][l:l + 1], p["pre_w1"], p["pre_w3"], p["pre_w2"], l)
        if l % 2 == 0:
            lam_init = 0.8 - 0.6 * math.exp(-0.3 * l)
            q, k, v = _qkv(x, p["ln_mix"][l:l + 1], p["wqkv"], j, tabs, period_rows)
            o = attn_fn(j, q, k, v, p["lp"][j], p["attn_subln"][j:j + 1], lam_init)
            x = _mm_res(o, p["wo"], j, x)
            ks.append(k)
            vs.append(v)
        else:
            gate, u = _rec_in(x, p["ln_mix"][l:l + 1], p["w_gate_in"], p["b_gate_in"], j)
            uc, nb = _conv(u.reshape(bsz, t_len, D_RNN), buf0[j], p["conv_w"], p["conv_b"], j)
            a, xin = _gates(uc.reshape(bsz * t_len, D_RNN), p["wa_d"], p["wx_d"],
                            p["ba"], p["bx"], p["lam"], j)
            yg, ht = _scan(a.reshape(bsz, t_len, D_RNN), xin.reshape(bsz, t_len, D_RNN),
                           gate.reshape(bsz, t_len, D_RNN), h0[j][:, None, :])
            x = _mm_res(yg.reshape(bsz * t_len, D_RNN), p["w_out"], j, x)
            hs.append(ht[:, 0, :])
            bufs.append(nb)
        gf = p["ln_final"] if l == DEPTH - 1 else None
        x = _ffn(x, p["ln_ffn_post"][l:l + 1], p["post_w1"], p["post_w3"], p["post_w2"], l, gf)
    return x, ks, vs, hs, bufs


def kernel(x_prompt, x_sample, cache_k, cache_v, state_h, state_conv, page_table, ln_ffn_pre, ffn_pre_w1, ffn_pre_w3, ffn_pre_w2, ln_mix, ln_ffn_post, ffn_post_w1, ffn_post_w3, ffn_post_w2, attn_wq, attn_wk, attn_wv, attn_lq1, attn_lk1, attn_lq2, attn_lk2, attn_subln, attn_wo, rec_w_gate, rec_b_gate, rec_w_in, rec_b_in, rec_conv_w, rec_conv_b, rec_wa, rec_ba, rec_wx, rec_bx, rec_lam, rec_w_out, ln_final):
    n_b = rec_lam.shape[0]
    bsz, seq = x_prompt.shape[0], x_prompt.shape[1]
    db, dec_seq = x_sample.shape[0], x_sample.shape[1]
    past = page_table.shape[1] * cache_k.shape[2]

    p = {
        "ln_ffn_pre": ln_ffn_pre, "ln_mix": ln_mix, "ln_ffn_post": ln_ffn_post,
        "ln_final": ln_final[None, :],
        "pre_w1": ffn_pre_w1.astype(BF16), "pre_w3": ffn_pre_w3.astype(BF16),
        "pre_w2": ffn_pre_w2.astype(BF16),
        "post_w1": ffn_post_w1.astype(BF16), "post_w3": ffn_post_w3.astype(BF16),
        "post_w2": ffn_post_w2.astype(BF16),
        "wqkv": jnp.concatenate([attn_wq, attn_wk, attn_wv], axis=-1).astype(BF16),
        "lp": jnp.stack([attn_lq1, attn_lk1, attn_lq2, attn_lk2], axis=1).astype(F32),
        "attn_subln": attn_subln, "wo": attn_wo.astype(BF16),
        "w_gate_in": jnp.concatenate([rec_w_gate, rec_w_in], axis=-1).astype(BF16),
        "b_gate_in": jnp.concatenate([rec_b_gate, rec_b_in], axis=-1)[:, None, :],
        "conv_w": rec_conv_w, "conv_b": rec_conv_b[:, None, :],
        "wa_d": _block_diag_dense(rec_wa).astype(BF16), "wx_d": _block_diag_dense(rec_wx).astype(BF16),
        "ba": rec_ba[:, None, :], "bx": rec_bx[:, None, :], "lam": rec_lam[:, None, :],
        "w_out": rec_w_out.astype(BF16),
    }

    def prompt_mix(j, q, k, v, lp, g, lam_init):
        return _prompt_attention(q, k, v, lp, g, bsz, seq, lam_init)

    h0_p = jnp.zeros((n_b, bsz, D_RNN), F32)
    buf0_p = jnp.zeros((n_b, bsz, CONV_W - 1, D_RNN), F32)
    y_p, k_p, v_p, h_p, c_p = _trunk(
        x_prompt.reshape(bsz * seq, D_MODEL), bsz, seq, p,
        _rope_tables(jnp.arange(seq)), seq, prompt_mix, h0_p, buf0_p)

    def sample_mix(j, q, k, v, lp, g, lam_init):
        q5 = q.reshape(db, dec_seq, N_HEADS, 2, HEAD_DIM)
        zero = jnp.zeros_like(q5[:, :, :, 0])
        q1 = jnp.concatenate([q5[:, :, :, 0], zero], axis=-1)
        q2 = jnp.concatenate([zero, q5[:, :, :, 1]], axis=-1)
        qall = jnp.stack([q1, q2], axis=2).reshape(db, dec_seq * 2 * N_HEADS, V_DIM)
        o = _sample_attention(qall, k.reshape(db, dec_seq * N_HEADS, V_DIM),
                              v.reshape(db, dec_seq * N_HEADS, V_DIM), cache_k, cache_v, j,
                              page_table, lp, g, lam_init)
        return o.reshape(db * dec_seq, ATTN_WIDTH)

    pos_s = past + jnp.tile(jnp.arange(dec_seq), db)
    y_s, k_s, v_s, h_s, c_s = _trunk(
        x_sample.reshape(db * dec_seq, D_MODEL), db, dec_seq, p,
        _rope_tables(pos_s), db * dec_seq, sample_mix, state_h, state_conv)

    def kv(xs, b_, t_):
        return jnp.stack(xs).reshape(len(xs), b_, t_, N_HEADS, V_DIM)

    return (y_p.reshape(bsz, seq, D_MODEL), y_s.reshape(db, dec_seq, D_MODEL),
            kv(k_p, bsz, seq), kv(v_p, bsz, seq), jnp.stack(h_p), jnp.stack(c_p),
            kv(k_s, db, dec_seq), kv(v_s, db, dec_seq), jnp.stack(h_s), jnp.stack(c_s))
```

```python
import functools
import math

import jax
import jax.numpy as jnp
from jax import lax
from jax.experimental import pallas as pl
from jax.experimental.pallas import tpu as pltpu

F32 = jnp.float32
BF16 = jnp.bfloat16

D_MODEL = 2048
DEPTH = 4
N_HEADS = 16
HEAD_DIM = 64
V_DIM = 2 * HEAD_DIM
ATTN_WIDTH = N_HEADS * V_DIM
ROT_DIM = HEAD_DIM // 4
ROPE_THETA = 500000.0
D_RNN = 2688
RG_HEADS = 16
RG_BLOCK = D_RNN // RG_HEADS
CONV_W = 4
RG_C = 8.0
D_FF = 5632
EPS = 1e-6

LANES = 128
SUBLANES = 8
VMEM_LIMIT_BYTES = 56 * 1024 * 1024
NEG_INF = float("-inf")


def _params(*sem):
    return pltpu.CompilerParams(dimension_semantics=sem, vmem_limit_bytes=VMEM_LIMIT_BYTES)


def _rms(x, g):
    ms = jnp.mean(x * x, axis=-1, keepdims=True)
    return x * lax.rsqrt(ms + EPS) * g


def _dot(a, b):
    return jnp.dot(a, b, preferred_element_type=F32)


def _dot_nt(a, b):
    return lax.dot_general(a, b, (((1,), (1,)), ((), ())), preferred_element_type=F32)


def _row_tile(m, cap):
    return cap if m % cap == 0 else m


def _ffn_body(x_ref, g_ref, w1_ref, w3_ref, w2_ref, gf_ref, o_ref, xn_ref, *, n_f, final_norm):
    j = pl.program_id(1)

    @pl.when(j == 0)
    def _():
        xn_ref[...] = _rms(x_ref[...], g_ref[...]).astype(BF16)
        o_ref[...] = jnp.zeros_like(o_ref)

    xn = xn_ref[...]
    a = _dot(xn, w1_ref[...])
    b = _dot(xn, w3_ref[...])
    h = (jax.nn.silu(a) * b).astype(BF16)
    o_ref[...] += _dot(h, w2_ref[...])

    @pl.when(j == n_f - 1)
    def _():
        y = x_ref[...] + 0.5 * o_ref[...]
        if final_norm:
            y = _rms(y, gf_ref[...])
        o_ref[...] = y


def _ffn(x, g, w1, w3, w2, layer, gf=None):
    m = x.shape[0]
    tm = _row_tile(m, 1024)
    tf = 512
    n_f = D_FF // tf
    final_norm = gf is not None
    if gf is None:
        gf = g
    body = functools.partial(_ffn_body, n_f=n_f, final_norm=final_norm)
    return pl.pallas_call(
        body,
        grid=(m // tm, n_f),
        in_specs=[
            pl.BlockSpec((tm, D_MODEL), lambda i, j: (i, 0), pipeline_mode=pl.Buffered(1)),
            pl.BlockSpec((1, D_MODEL), lambda i, j: (0, 0)),
            pl.BlockSpec((None, D_MODEL, tf), lambda i, j: (layer, 0, j)),
            pl.BlockSpec((None, D_MODEL, tf), lambda i, j: (layer, 0, j)),
            pl.BlockSpec((None, tf, D_MODEL), lambda i, j: (layer, j, 0)),
            pl.BlockSpec((1, D_MODEL), lambda i, j: (0, 0)),
        ],
        out_specs=pl.BlockSpec((tm, D_MODEL), lambda i, j: (i, 0)),
        out_shape=jax.ShapeDtypeStruct((m, D_MODEL), F32),
        scratch_shapes=[pltpu.VMEM((tm, D_MODEL), BF16)],
        compiler_params=_params("arbitrary", "arbitrary"),
        name="ffn",
    )(x, g, w1, w3, w2, gf)


def _mm_res_body(a_ref, w_ref, r_ref, o_ref, *scratch):
    if scratch:
        ab_ref, = scratch

        @pl.when(pl.program_id(1) == 0)
        def _():
            ab_ref[...] = a_ref[...].astype(BF16)
    else:
        ab_ref = a_ref

    o_ref[...] = r_ref[...] + _dot(ab_ref[...], w_ref[...])


def _mm_res(a, w, layer, resid):
    m, k = a.shape
    n = w.shape[-1]
    tm = _row_tile(m, 1024)
    tn = 1024
    scratch = [] if a.dtype == BF16 else [pltpu.VMEM((tm, k), BF16)]
    return pl.pallas_call(
        _mm_res_body,
        grid=(m // tm, n // tn),
        in_specs=[
            pl.BlockSpec((tm, k), lambda i, j: (i, 0)),
            pl.BlockSpec((None, k, tn), lambda i, j: (layer, 0, j)),
            pl.BlockSpec((tm, tn), lambda i, j: (i, j)),
        ],
        out_specs=pl.BlockSpec((tm, tn), lambda i, j: (i, j)),
        out_shape=jax.ShapeDtypeStruct((m, n), F32),
        scratch_shapes=scratch,
        compiler_params=_params("arbitrary", "arbitrary"),
        name="mm_res",
    )(a, w, resid)


def _qkv_body(x_ref, g_ref, w_ref, c_ref, sa_ref, sb_ref, q_ref, k_ref, v_ref, xn_ref, *, nq, tn):
    j = pl.program_id(1)

    @pl.when(j == 0)
    def _():
        xn_ref[...] = _rms(x_ref[...], g_ref[...]).astype(BF16)

    acc = _dot(xn_ref[...], w_ref[...])

    def rope(y):
        c, sa, sb = c_ref[...], sa_ref[...], sb_ref[...]
        outs = []
        for ch in range(tn // LANES):
            yc = y[:, ch * LANES:(ch + 1) * LANES]
            outs.append(yc * c + pltpu.roll(yc, LANES - ROT_DIM // 2, 1) * sa
                        + pltpu.roll(yc, ROT_DIM // 2, 1) * sb)
        return jnp.concatenate(outs, axis=1)

    @pl.when(j < nq)
    def _():
        q_ref[...] = (rope(acc) * (HEAD_DIM ** -0.5)).astype(BF16)

    @pl.when((j >= nq) & (j < 2 * nq))
    def _():
        k_ref[...] = rope(acc)

    @pl.when(j >= 2 * nq)
    def _():
        v_ref[...] = acc


def _qkv(x, g, wqkv, layer, tabs, period_rows):
    m = x.shape[0]
    tm = _row_tile(m, 512)
    tn = 1024
    nq = ATTN_WIDTH // tn
    pb = period_rows // tm
    body = functools.partial(_qkv_body, nq=nq, tn=tn)
    tab_spec = pl.BlockSpec((tm, LANES), lambda i, j: (i % pb, 0))
    return pl.pallas_call(
        body,
        grid=(m // tm, 3 * nq),
        in_specs=[
            pl.BlockSpec((tm, D_MODEL), lambda i, j: (i, 0)),
            pl.BlockSpec((1, D_MODEL), lambda i, j: (0, 0)),
            pl.BlockSpec((None, D_MODEL, tn), lambda i, j: (layer, 0, j)),
            tab_spec, tab_spec, tab_spec,
        ],
        out_specs=[
            pl.BlockSpec((tm, tn), lambda i, j: (i, jnp.minimum(j, nq - 1))),
            pl.BlockSpec((tm, tn), lambda i, j: (i, jnp.clip(j - nq, 0, nq - 1))),
            pl.BlockSpec((tm, tn), lambda i, j: (i, jnp.clip(j - 2 * nq, 0, nq - 1))),
        ],
        out_shape=[
            jax.ShapeDtypeStruct((m, ATTN_WIDTH), BF16),
            jax.ShapeDtypeStruct((m, ATTN_WIDTH), F32),
            jax.ShapeDtypeStruct((m, ATTN_WIDTH), F32),
        ],
        scratch_shapes=[pltpu.VMEM((tm, D_MODEL), BF16)],
        compiler_params=_params("arbitrary", "arbitrary"),
        name="qkv",
    )(x, g, wqkv, *tabs)


def _lam_from(lp):
    s1 = jnp.sum(lp[0:1, :] * lp[1:2, :], axis=-1, keepdims=True)
    s2 = jnp.sum(lp[2:3, :] * lp[3:4, :], axis=-1, keepdims=True)
    return jnp.exp(s1) - jnp.exp(s2)


def _pattn_body(lp_ref, g_ref, q_ref, k_ref, v_ref, o_ref, kb_ref, vb_ref, *, tq, lam_init):
    qi = pl.program_id(2)

    @pl.when(qi == 0)
    def _():
        kb_ref[...] = k_ref[...].astype(BF16)
        vb_ref[...] = v_ref[...].astype(BF16)

    q = q_ref[...]
    lane = lax.broadcasted_iota(jnp.int32, (tq, V_DIM), 1)
    zero = jnp.zeros_like(q)
    qs = jnp.concatenate([jnp.where(lane < HEAD_DIM, q, zero),
                          jnp.where(lane >= HEAD_DIM, q, zero)], axis=0)

    def scores(j):
        kt = kb_ref[pl.ds(pl.multiple_of(j * tq, tq), tq), :]
        return _dot_nt(qs, kt)

    def values(j):
        return vb_ref[pl.ds(pl.multiple_of(j * tq, tq), tq), :]

    row = lax.broadcasted_iota(jnp.int32, (2 * tq, tq), 0)
    col = lax.broadcasted_iota(jnp.int32, (2 * tq, tq), 1)
    qrow = jnp.where(row >= tq, row - tq, row)
    s = jnp.where(col <= qrow, scores(qi), NEG_INF)
    m0 = jnp.max(s, axis=-1, keepdims=True)
    p = jnp.exp(s - m0)
    l0 = jnp.sum(p, axis=-1, keepdims=True)
    acc0 = _dot(p.astype(BF16), values(qi))

    def body(j, carry):
        m, l, acc = carry
        s = scores(j)
        m_new = jnp.maximum(m, jnp.max(s, axis=-1, keepdims=True))
        alpha = jnp.exp(m - m_new)
        p = jnp.exp(s - m_new)
        l = alpha * l + jnp.sum(p, axis=-1, keepdims=True)
        acc = alpha * acc + _dot(p.astype(BF16), values(j))
        return m_new, l, acc

    _, l, acc = lax.fori_loop(0, qi, body, (m0, l0, acc0))
    o = acc / l
    lam = _lam_from(lp_ref[...]) + lam_init
    o = o[:tq] - lam * o[tq:]
    o_ref[...] = (_rms(o, g_ref[...]) * (1.0 - lam_init)).astype(BF16)


def _prompt_attention(q, k, v, lp, g, batch, seq, lam_init):
    tq = 512
    nq = seq // tq
    body = functools.partial(_pattn_body, tq=tq, lam_init=lam_init)
    return pl.pallas_call(
        body,
        grid=(batch, N_HEADS, nq),
        in_specs=[
            pl.BlockSpec((4, HEAD_DIM), lambda b, h, i: (0, 0)),
            pl.BlockSpec((1, V_DIM), lambda b, h, i: (0, 0)),
            pl.BlockSpec((tq, V_DIM), lambda b, h, i: (b * nq + i, h)),
            pl.BlockSpec((seq, V_DIM), lambda b, h, i: (b, h)),
            pl.BlockSpec((seq, V_DIM), lambda b, h, i: (b, h)),
        ],
        out_specs=pl.BlockSpec((tq, V_DIM), lambda b, h, i: (b * nq + i, h)),
        out_shape=jax.ShapeDtypeStruct((batch * seq, ATTN_WIDTH), BF16),
        scratch_shapes=[pltpu.VMEM((seq, V_DIM), BF16), pltpu.VMEM((seq, V_DIM), BF16)],
        compiler_params=_params("arbitrary", "arbitrary", "arbitrary"),
        name="prompt_attn",
    )(lp, g, q, k, v)


HEAD_GROUP = SUBLANES
N_GROUPS = N_HEADS // HEAD_GROUP
PAGES_PER_STEP = 4


def _dattn_body(pt_ref, lp_ref, g_ref, q_ref, kn_ref, vn_ref, *rest,
                n_steps, page, dec_seq, lam_init):
    del pt_ref
    kp_refs = rest[:PAGES_PER_STEP]
    vp_refs = rest[PAGES_PER_STEP:2 * PAGES_PER_STEP]
    o_ref, m_ref, l_ref, acc_ref, bias_ref = rest[2 * PAGES_PER_STEP:]
    step = pl.program_id(1)
    rows = 2 * dec_seq * HEAD_GROUP

    def group(ref, gi):
        x = ref[:, gi * HEAD_GROUP:(gi + 1) * HEAD_GROUP, :]
        return x.reshape(x.shape[0] * HEAD_GROUP, V_DIM).astype(BF16)

    def qg(gi):
        return q_ref[gi * rows:(gi + 1) * rows, :]

    @pl.when(step == 0)
    def _():
        r = lax.broadcasted_iota(jnp.int32, (rows, page * HEAD_GROUP), 0)
        c = lax.broadcasted_iota(jnp.int32, (rows, page * HEAD_GROUP), 1)
        same_head = (r & (HEAD_GROUP - 1)) == (c & (HEAD_GROUP - 1))
        bias_ref[...] = jnp.where(same_head, 0.0, NEG_INF)
        nk = dec_seq * HEAD_GROUP
        r2 = lax.broadcasted_iota(jnp.int32, (rows, nk), 0)
        c2 = lax.broadcasted_iota(jnp.int32, (rows, nk), 1)
        ok = ((r2 & (HEAD_GROUP - 1)) == (c2 & (HEAD_GROUP - 1))) & (
            lax.shift_right_logical(c2, 3) <= lax.shift_right_logical(r2, 4))
        for gi in range(N_GROUPS):
            sl = slice(gi * rows, (gi + 1) * rows)
            s = jnp.where(ok, _dot_nt(qg(gi), group(kn_ref, gi)), NEG_INF)
            m = jnp.max(s, axis=-1, keepdims=True)
            pr = jnp.exp(s - m)
            m_ref[sl, :] = m
            l_ref[sl, :] = jnp.sum(pr, axis=-1, keepdims=True)
            acc_ref[sl, :] = _dot(pr.astype(BF16), group(vn_ref, gi))

    bias = bias_ref[...]
    for gi in range(N_GROUPS):
        sl = slice(gi * rows, (gi + 1) * rows)
        q = qg(gi)
        ss = [_dot_nt(q, group(kp, gi)) + bias for kp in kp_refs]
        m_old = m_ref[sl, :]
        m_new = m_old
        for s in ss:
            m_new = jnp.maximum(m_new, jnp.max(s, axis=-1, keepdims=True))
        alpha = jnp.exp(m_old - m_new)
        l_new = alpha * l_ref[sl, :]
        acc = alpha * acc_ref[sl, :]
        for s, vp in zip(ss, vp_refs):
            pr = jnp.exp(s - m_new)
            l_new = l_new + jnp.sum(pr, axis=-1, keepdims=True)
            acc = acc + _dot(pr.astype(BF16), group(vp, gi))
        m_ref[sl, :] = m_new
        l_ref[sl, :] = l_new
        acc_ref[sl, :] = acc

    @pl.when(step == n_steps - 1)
    def _():
        o = acc_ref[...] / l_ref[...]
        lam = _lam_from(lp_ref[...]) + lam_init
        g = g_ref[...]
        for gi in range(N_GROUPS):
            for t in range(dec_seq):
                base = gi * rows + 2 * t * HEAD_GROUP
                o1 = o[base:base + HEAD_GROUP]
                o2 = o[base + HEAD_GROUP:base + 2 * HEAD_GROUP]
                o_ref[t, gi * HEAD_GROUP:(gi + 1) * HEAD_GROUP, :] = (
                    _rms(o1 - lam * o2, g) * (1.0 - lam_init))


def _sample_attention(qall, k_new, v_new, cache_k, cache_v, slot, page_table, lp, g, lam_init):
    db, n_pages = page_table.shape
    page = cache_k.shape[2]
    dec_seq = k_new.shape[1]
    nq = 2 * dec_seq * N_HEADS
    n_steps = n_pages // PAGES_PER_STEP
    body = functools.partial(_dattn_body, n_steps=n_steps, page=page, dec_seq=dec_seq,
                             lam_init=lam_init)

    def page_spec(i):
        return pl.BlockSpec(
            (None, None, page, N_HEADS, V_DIM),
            lambda b, s, pt: (slot, pt[b * n_pages + s * PAGES_PER_STEP + i], 0, 0, 0))

    page_specs = [page_spec(i) for i in range(PAGES_PER_STEP)]
    new_spec = pl.BlockSpec((None, dec_seq, N_HEADS, V_DIM), lambda b, s, pt: (b, 0, 0, 0))
    grid_spec = pltpu.PrefetchScalarGridSpec(
        num_scalar_prefetch=1,
        grid=(db, n_steps),
        in_specs=[
            pl.BlockSpec((4, HEAD_DIM), lambda b, s, pt: (0, 0)),
            pl.BlockSpec((1, V_DIM), lambda b, s, pt: (0, 0)),
            pl.BlockSpec((None, nq, V_DIM), lambda b, s, pt: (b, 0, 0)),
            new_spec, new_spec, *page_specs, *page_specs,
        ],
        out_specs=pl.BlockSpec((None, dec_seq, N_HEADS, V_DIM), lambda b, s, pt: (b, 0, 0, 0)),
        scratch_shapes=[
            pltpu.VMEM((nq, 1), F32), pltpu.VMEM((nq, 1), F32), pltpu.VMEM((nq, V_DIM), F32),
            pltpu.VMEM((nq // N_GROUPS, page * HEAD_GROUP), F32),
        ],
    )
    return pl.pallas_call(
        body,
        grid_spec=grid_spec,
        out_shape=jax.ShapeDtypeStruct((db, dec_seq, N_HEADS, V_DIM), F32),
        compiler_params=_params("arbitrary", "arbitrary"),
        name="sample_attn",
    )(page_table.reshape(-1), lp, g, qall, k_new, v_new,
      *([cache_k] * PAGES_PER_STEP), *([cache_v] * PAGES_PER_STEP))


def _recin_body(x_ref, g_ref, w_ref, b_ref, gate_ref, u_ref, xn_ref, *, nh):
    j = pl.program_id(1)

    @pl.when(j == 0)
    def _():
        xn_ref[...] = _rms(x_ref[...], g_ref[...]).astype(BF16)

    acc = _dot(xn_ref[...], w_ref[...]) + b_ref[...]

    @pl.when(j < nh)
    def _():
        gate_ref[...] = jax.nn.gelu(acc)

    @pl.when(j >= nh)
    def _():
        u_ref[...] = acc


def _rec_in(x, g, w_cat, b_cat, layer):
    m = x.shape[0]
    tm = _row_tile(m, 1024)
    tn = 896
    nh = D_RNN // tn
    body = functools.partial(_recin_body, nh=nh)
    return pl.pallas_call(
        body,
        grid=(m // tm, 2 * nh),
        in_specs=[
            pl.BlockSpec((tm, D_MODEL), lambda i, j: (i, 0)),
            pl.BlockSpec((1, D_MODEL), lambda i, j: (0, 0)),
            pl.BlockSpec((None, D_MODEL, tn), lambda i, j: (layer, 0, j)),
            pl.BlockSpec((None, 1, tn), lambda i, j: (layer, 0, j)),
        ],
        out_specs=[
            pl.BlockSpec((tm, tn), lambda i, j: (i, jnp.minimum(j, nh - 1))),
            pl.BlockSpec((tm, tn), lambda i, j: (i, jnp.maximum(j - nh, 0))),
        ],
        out_shape=[jax.ShapeDtypeStruct((m, D_RNN), F32), jax.ShapeDtypeStruct((m, D_RNN), F32)],
        scratch_shapes=[pltpu.VMEM((tm, D_MODEL), BF16)],
        compiler_params=_params("arbitrary", "arbitrary"),
        name="rec_in",
    )(x, g, w_cat, b_cat)


def _conv_body(u_ref, buf_ref, w_ref, b_ref, uc_ref, nb_ref, pad_ref, *, t_len):
    lo = SUBLANES - (CONV_W - 1)
    pad_ref[lo:SUBLANES, :] = buf_ref[...]
    pad_ref[SUBLANES:SUBLANES + t_len, :] = u_ref[...]
    out = b_ref[...] + pad_ref[lo:lo + t_len, :] * w_ref[0:1, :]
    for j in range(1, CONV_W):
        out = out + pad_ref[lo + j:lo + j + t_len, :] * w_ref[j:j + 1, :]
    uc_ref[...] = out
    nb_ref[...] = pad_ref[t_len + lo:t_len + SUBLANES, :]


def _conv(u, buf, w, b, layer):
    bsz, t_len, _ = u.shape
    tc = 384
    body = functools.partial(_conv_body, t_len=t_len)
    return pl.pallas_call(
        body,
        grid=(bsz, D_RNN // tc),
        in_specs=[
            pl.BlockSpec((None, t_len, tc), lambda b_, c: (b_, 0, c)),
            pl.BlockSpec((None, CONV_W - 1, tc), lambda b_, c: (b_, 0, c)),
            pl.BlockSpec((None, CONV_W, tc), lambda b_, c: (layer, 0, c)),
            pl.BlockSpec((None, 1, tc), lambda b_, c: (layer, 0, c)),
        ],
        out_specs=[
            pl.BlockSpec((None, t_len, tc), lambda b_, c: (b_, 0, c)),
            pl.BlockSpec((None, CONV_W - 1, tc), lambda b_, c: (b_, 0, c)),
        ],
        out_shape=[jax.ShapeDtypeStruct(u.shape, F32), jax.ShapeDtypeStruct(buf.shape, F32)],
        scratch_shapes=[pltpu.VMEM((t_len + SUBLANES, tc), F32)],
        compiler_params=_params("arbitrary", "arbitrary"),
        name="conv",
    )(u, buf, w, b)


def _softplus(z):
    return jnp.maximum(z, 0.0) + jnp.log1p(jnp.exp(-jnp.abs(z)))


GATE_TN = 3 * LANES
GATE_TK = 7 * LANES


def _gate_window(j):
    first_block = (GATE_TN * j) // RG_BLOCK
    k0 = (RG_BLOCK * first_block) // LANES * LANES
    smallest = min if isinstance(j, int) else jnp.minimum
    return smallest(k0, D_RNN - GATE_TK)


def _banded(w):
    n_l, n_h, bi, bj = w.shape
    w = w.astype(BF16)
    dense = jnp.zeros((n_l, n_h * bi, n_h * bj), BF16)
    for h in range(n_h):
        dense = lax.dynamic_update_slice(dense, w[:, h], (0, h * bi, h * bj))
    tiles = []
    for j in range(D_RNN // GATE_TN):
        k0 = _gate_window(j)
        lo_block = (GATE_TN * j) // RG_BLOCK
        hi_block = (GATE_TN * (j + 1) - 1) // RG_BLOCK
        assert k0 <= RG_BLOCK * lo_block and RG_BLOCK * (hi_block + 1) <= k0 + GATE_TK
        tiles.append(dense[:, k0:k0 + GATE_TK, GATE_TN * j:GATE_TN * (j + 1)])
    return jnp.stack(tiles, axis=1)


def _gates_body(uc_ref, wa_ref, wx_ref, ba_ref, bx_ref, lam_ref, a_ref, xin_ref, ub_ref):
    j = pl.program_id(1)

    @pl.when(j == 0)
    def _():
        ub_ref[...] = uc_ref[...].astype(BF16)

    ub = ub_ref[:, pl.ds(pl.multiple_of(_gate_window(j), LANES), GATE_TK)]
    r = jax.nn.sigmoid(_dot(ub, wa_ref[...]) + ba_ref[...])
    i = jax.nn.sigmoid(_dot(ub, wx_ref[...]) + bx_ref[...])
    log_a = -RG_C * r * _softplus(-lam_ref[...])
    a = jnp.exp(log_a)
    u = uc_ref[:, pl.ds(pl.multiple_of(j * GATE_TN, LANES), GATE_TN)]
    a_ref[...] = a
    xin_ref[...] = jnp.sqrt(1.0 - a * a) * i * u


def _gates(uc, wa_b, wx_b, ba, bx, lam, layer):
    m = uc.shape[0]
    tm = _row_tile(m, 1024)
    vec = pl.BlockSpec((None, 1, GATE_TN), lambda i, j: (layer, 0, j))
    wsp = pl.BlockSpec((None, None, GATE_TK, GATE_TN), lambda i, j: (layer, j, 0, 0))
    osp = pl.BlockSpec((tm, GATE_TN), lambda i, j: (i, j))
    return pl.pallas_call(
        _gates_body,
        grid=(m // tm, D_RNN // GATE_TN),
        in_specs=[pl.BlockSpec((tm, D_RNN), lambda i, j: (i, 0)), wsp, wsp, vec, vec, vec],
        out_specs=[osp, osp],
        out_shape=[jax.ShapeDtypeStruct((m, D_RNN), F32), jax.ShapeDtypeStruct((m, D_RNN), F32)],
        scratch_shapes=[pltpu.VMEM((tm, D_RNN), BF16)],
        compiler_params=_params("arbitrary", "arbitrary"),
        name="gates",
    )(uc, wa_b, wx_b, ba, bx, lam)


def _scan_body(a_ref, x_ref, gate_ref, h0_ref, y_ref, ht_ref, *, t_len):
    tc = a_ref.shape[-1]
    h = h0_ref[...]
    if t_len % SUBLANES == 0:
        row = lax.broadcasted_iota(jnp.int32, (SUBLANES, tc), 0)

        def body(c, h):
            sl = pl.ds(pl.multiple_of(c * SUBLANES, SUBLANES), SUBLANES)
            a = a_ref[sl, :]
            x = x_ref[sl, :]
            d = 1
            while d < SUBLANES:
                valid = row >= d
                x = jnp.where(valid, a * pltpu.roll(x, d, 0) + x, x)
                a = jnp.where(valid, a * pltpu.roll(a, d, 0), a)
                d *= 2
            hs = a * h + x
            y_ref[sl, :] = hs * gate_ref[sl, :]
            return hs[SUBLANES - 1:SUBLANES, :]

        h = lax.fori_loop(0, t_len // SUBLANES, body, h)
    else:
        for t in range(t_len):
            h = a_ref[t:t + 1, :] * h + x_ref[t:t + 1, :]
            y_ref[t:t + 1, :] = h * gate_ref[t:t + 1, :]
    ht_ref[...] = h


def _scan(a, xin, gate, h0):
    bsz, t_len, _ = a.shape
    tc = 384
    body = functools.partial(_scan_body, t_len=t_len)
    seq_spec = pl.BlockSpec((None, t_len, tc), lambda b_, c: (b_, 0, c))
    h_spec = pl.BlockSpec((None, 1, tc), lambda b_, c: (b_, 0, c))
    return pl.pallas_call(
        body,
        grid=(bsz, D_RNN // tc),
        in_specs=[seq_spec, seq_spec, seq_spec, h_spec],
        out_specs=[seq_spec, h_spec],
        out_shape=[jax.ShapeDtypeStruct(a.shape, F32), jax.ShapeDtypeStruct((bsz, 1, D_RNN), F32)],
        compiler_params=_params("arbitrary", "arbitrary"),
        name="scan",
    )(a, xin, gate, h0)


def _rope_tables(pos):
    half = ROT_DIM // 2
    inv = ROPE_THETA ** (-jnp.arange(0, ROT_DIM, 2, dtype=F32) / ROT_DIM)
    ang = pos.astype(F32)[:, None] * inv[None, :]
    cos, sin = jnp.cos(ang), jnp.sin(ang)
    t_len = pos.shape[0]
    ones = jnp.ones((t_len, HEAD_DIM - ROT_DIM), F32)
    zer = jnp.zeros((t_len, HEAD_DIM - ROT_DIM), F32)
    zh = jnp.zeros((t_len, half), F32)
    c = jnp.concatenate([cos, cos, ones], axis=1)
    sa = jnp.concatenate([-sin, zh, zer], axis=1)
    sb = jnp.concatenate([zh, sin, zer], axis=1)
    return tuple(jnp.concatenate([t, t], axis=1) for t in (c, sa, sb))


def _trunk(x, bsz, t_len, p, tabs, period_rows, attn_fn, h0, buf0):
    ks, vs, hs, bufs = [], [], [], []
    for l in range(DEPTH):
        j = l // 2
        x = _ffn(x, p["ln_ffn_pre"][l:l + 1], p["pre_w1"], p["pre_w3"], p["pre_w2"], l)
        if l % 2 == 0:
            lam_init = 0.8 - 0.6 * math.exp(-0.3 * l)
            q, k, v = _qkv(x, p["ln_mix"][l:l + 1], p["wqkv"], j, tabs, period_rows)
            o = attn_fn(j, q, k, v, p["lp"][j], p["attn_subln"][j:j + 1], lam_init)
            x = _mm_res(o, p["wo"], j, x)
            ks.append(k)
            vs.append(v)
        else:
            gate, u = _rec_in(x, p["ln_mix"][l:l + 1], p["w_gate_in"], p["b_gate_in"], j)
            uc, nb = _conv(u.reshape(bsz, t_len, D_RNN), buf0[j], p["conv_w"], p["conv_b"], j)
            a, xin = _gates(uc.reshape(bsz * t_len, D_RNN), p["wa_b"], p["wx_b"],
                            p["ba"], p["bx"], p["lam"], j)
            yg, ht = _scan(a.reshape(bsz, t_len, D_RNN), xin.reshape(bsz, t_len, D_RNN),
                           gate.reshape(bsz, t_len, D_RNN), h0[j][:, None, :])
            x = _mm_res(yg.reshape(bsz * t_len, D_RNN), p["w_out"], j, x)
            hs.append(ht[:, 0, :])
            bufs.append(nb)
        gf = p["ln_final"] if l == DEPTH - 1 else None
        x = _ffn(x, p["ln_ffn_post"][l:l + 1], p["post_w1"], p["post_w3"], p["post_w2"], l, gf)
    return x, ks, vs, hs, bufs


def kernel(x_prompt, x_sample, cache_k, cache_v, state_h, state_conv, page_table, ln_ffn_pre, ffn_pre_w1, ffn_pre_w3, ffn_pre_w2, ln_mix, ln_ffn_post, ffn_post_w1, ffn_post_w3, ffn_post_w2, attn_wq, attn_wk, attn_wv, attn_lq1, attn_lk1, attn_lq2, attn_lk2, attn_subln, attn_wo, rec_w_gate, rec_b_gate, rec_w_in, rec_b_in, rec_conv_w, rec_conv_b, rec_wa, rec_ba, rec_wx, rec_bx, rec_lam, rec_w_out, ln_final):
    n_b = rec_lam.shape[0]
    bsz, seq = x_prompt.shape[0], x_prompt.shape[1]
    db, dec_seq = x_sample.shape[0], x_sample.shape[1]
    past = page_table.shape[1] * cache_k.shape[2]

    p = {
        "ln_ffn_pre": ln_ffn_pre, "ln_mix": ln_mix, "ln_ffn_post": ln_ffn_post,
        "ln_final": ln_final[None, :],
        "pre_w1": ffn_pre_w1.astype(BF16), "pre_w3": ffn_pre_w3.astype(BF16),
        "pre_w2": ffn_pre_w2.astype(BF16),
        "post_w1": ffn_post_w1.astype(BF16), "post_w3": ffn_post_w3.astype(BF16),
        "post_w2": ffn_post_w2.astype(BF16),
        "wqkv": jnp.concatenate([attn_wq, attn_wk, attn_wv], axis=-1).astype(BF16),
        "lp": jnp.stack([attn_lq1, attn_lk1, attn_lq2, attn_lk2], axis=1).astype(F32),
        "attn_subln": attn_subln, "wo": attn_wo.astype(BF16),
        "w_gate_in": jnp.concatenate([rec_w_gate, rec_w_in], axis=-1).astype(BF16),
        "b_gate_in": jnp.concatenate([rec_b_gate, rec_b_in], axis=-1)[:, None, :],
        "conv_w": rec_conv_w, "conv_b": rec_conv_b[:, None, :],
        "wa_b": _banded(rec_wa), "wx_b": _banded(rec_wx),
        "ba": rec_ba[:, None, :], "bx": rec_bx[:, None, :], "lam": rec_lam[:, None, :],
        "w_out": rec_w_out.astype(BF16),
    }

    def prompt_mix(j, q, k, v, lp, g, lam_init):
        return _prompt_attention(q, k, v, lp, g, bsz, seq, lam_init)

    h0_p = jnp.zeros((n_b, bsz, D_RNN), F32)
    buf0_p = jnp.zeros((n_b, bsz, CONV_W - 1, D_RNN), F32)
    y_p, k_p, v_p, h_p, c_p = _trunk(
        x_prompt.reshape(bsz * seq, D_MODEL), bsz, seq, p,
        _rope_tables(jnp.arange(seq)), seq, prompt_mix, h0_p, buf0_p)

    def sample_mix(j, q, k, v, lp, g, lam_init):
        q5 = q.reshape(db, dec_seq, N_HEADS, 2, HEAD_DIM)
        zero = jnp.zeros_like(q5[:, :, :, 0])
        q1 = jnp.concatenate([q5[:, :, :, 0], zero], axis=-1)
        q2 = jnp.concatenate([zero, q5[:, :, :, 1]], axis=-1)
        qall = jnp.stack([q1, q2], axis=2).reshape(db, dec_seq, 2, N_GROUPS, HEAD_GROUP, V_DIM)
        qall = qall.transpose(0, 3, 1, 2, 4, 5).reshape(db, dec_seq * 2 * N_HEADS, V_DIM)
        o = _sample_attention(qall, k.reshape(db, dec_seq, N_HEADS, V_DIM),
                              v.reshape(db, dec_seq, N_HEADS, V_DIM), cache_k, cache_v, j,
                              page_table, lp, g, lam_init)
        return o.reshape(db * dec_seq, ATTN_WIDTH)

    pos_s = past + jnp.tile(jnp.arange(dec_seq), db)
    y_s, k_s, v_s, h_s, c_s = _trunk(
        x_sample.reshape(db * dec_seq, D_MODEL), db, dec_seq, p,
        _rope_tables(pos_s), db * dec_seq, sample_mix, state_h, state_conv)

    def kv(xs, b_, t_):
        return jnp.stack(xs).reshape(len(xs), b_, t_, N_HEADS, V_DIM)

    return (y_p.reshape(bsz, seq, D_MODEL), y_s.reshape(db, dec_seq, D_MODEL),
            kv(k_p, bsz, seq), kv(v_p, bsz, seq), jnp.stack(h_p), jnp.stack(c_p),
            kv(k_s, db, dec_seq), kv(v_s, db, dec_seq), jnp.stack(h_s), jnp.stack(c_s))
```

```python
import functools
import math

import jax
import jax.numpy as jnp
from jax import lax
from jax.experimental import pallas as pl
from jax.experimental.pallas import tpu as pltpu

F32 = jnp.float32
BF16 = jnp.bfloat16

D_MODEL = 2048
DEPTH = 4
N_HEADS = 16
HEAD_DIM = 64
V_DIM = 2 * HEAD_DIM
ATTN_WIDTH = N_HEADS * V_DIM
ROT_DIM = HEAD_DIM // 4
ROPE_THETA = 500000.0
D_RNN = 2688
RG_HEADS = 16
RG_BLOCK = D_RNN // RG_HEADS
CONV_W = 4
RG_C = 8.0
D_FF = 5632
EPS = 1e-6

LANES = 128
SUBLANES = 8
VMEM_LIMIT_BYTES = 56 * 1024 * 1024
NEG_INF = float("-inf")


def _params(*sem):
    return pltpu.CompilerParams(dimension_semantics=sem, vmem_limit_bytes=VMEM_LIMIT_BYTES)


def _rms(x, g):
    ms = jnp.mean(x * x, axis=-1, keepdims=True)
    return x * lax.rsqrt(ms + EPS) * g


def _dot(a, b):
    return jnp.dot(a, b, preferred_element_type=F32)


def _dot_nt(a, b):
    return lax.dot_general(a, b, (((1,), (1,)), ((), ())), preferred_element_type=F32)


def _row_tile(m, cap):
    return cap if m % cap == 0 else m


def _ffn_body(x_ref, g_ref, w1_ref, w3_ref, w2_ref, gf_ref, o_ref, *rest, n_f, final_norm, emit):
    j = pl.program_id(1)
    xn_ref = rest[-1]

    @pl.when(j == 0)
    def _():
        xn_ref[...] = _rms(x_ref[...], g_ref[...]).astype(BF16)
        o_ref[...] = jnp.zeros_like(o_ref)

    w1, w3, w2 = w1_ref[...], w3_ref[...], w2_ref[...]
    if emit:
        w1, w3, w2 = w1.astype(BF16), w3.astype(BF16), w2.astype(BF16)
        rest[0][...], rest[1][...], rest[2][...] = w1, w3, w2
    xn = xn_ref[...]
    a = _dot(xn, w1)
    b = _dot(xn, w3)
    h = (jax.nn.silu(a) * b).astype(BF16)
    o_ref[...] += _dot(h, w2)

    @pl.when(j == n_f - 1)
    def _():
        y = x_ref[...] + 0.5 * o_ref[...]
        if final_norm:
            y = _rms(y, gf_ref[...])
        o_ref[...] = y


def _ffn(x, g, w1, w3, w2, gf=None, layer=None):
    m = x.shape[0]
    tm = _row_tile(m, 1024)
    tf = 512
    n_f = D_FF // tf
    emit = layer is not None
    assert not emit or m == tm
    final_norm = gf is not None
    if gf is None:
        gf = g
    body = functools.partial(_ffn_body, n_f=n_f, final_norm=final_norm, emit=emit)
    up_spec = pl.BlockSpec((D_MODEL, tf), lambda i, j: (0, j))
    down_spec = pl.BlockSpec((tf, D_MODEL), lambda i, j: (j, 0))
    out_specs = [pl.BlockSpec((tm, D_MODEL), lambda i, j: (i, 0))]
    out_shape = [jax.ShapeDtypeStruct((m, D_MODEL), F32)]
    w_specs = [up_spec, up_spec, down_spec]
    if emit:
        out_specs += w_specs
        out_shape += [jax.ShapeDtypeStruct(w.shape[1:], BF16) for w in (w1, w3, w2)]
        w_specs = [pl.BlockSpec((None, D_MODEL, tf), lambda i, j: (layer, 0, j)),
                   pl.BlockSpec((None, D_MODEL, tf), lambda i, j: (layer, 0, j)),
                   pl.BlockSpec((None, tf, D_MODEL), lambda i, j: (layer, j, 0))]
    outs = pl.pallas_call(
        body,
        grid=(m // tm, n_f),
        in_specs=[
            pl.BlockSpec((tm, D_MODEL), lambda i, j: (i, 0), pipeline_mode=pl.Buffered(1)),
            pl.BlockSpec((1, D_MODEL), lambda i, j: (0, 0)),
            *w_specs,
            pl.BlockSpec((1, D_MODEL), lambda i, j: (0, 0)),
        ],
        out_specs=out_specs,
        out_shape=out_shape,
        scratch_shapes=[pltpu.VMEM((tm, D_MODEL), BF16)],
        compiler_params=_params("arbitrary", "arbitrary"),
        name="ffn_cast" if emit else "ffn",
    )(x, g, w1, w3, w2, gf)
    return outs if emit else outs[0]


def _mm_res_body(a_ref, w_ref, r_ref, o_ref, *scratch):
    if scratch:
        ab_ref, = scratch

        @pl.when(pl.program_id(1) == 0)
        def _():
            ab_ref[...] = a_ref[...].astype(BF16)
    else:
        ab_ref = a_ref

    o_ref[...] = r_ref[...] + _dot(ab_ref[...], w_ref[...])


def _mm_res(a, w, layer, resid):
    m, k = a.shape
    n = w.shape[-1]
    tm = _row_tile(m, 1024)
    tn = 1024
    scratch = [] if a.dtype == BF16 else [pltpu.VMEM((tm, k), BF16)]
    return pl.pallas_call(
        _mm_res_body,
        grid=(m // tm, n // tn),
        in_specs=[
            pl.BlockSpec((tm, k), lambda i, j: (i, 0)),
            pl.BlockSpec((None, k, tn), lambda i, j: (layer, 0, j)),
            pl.BlockSpec((tm, tn), lambda i, j: (i, j)),
        ],
        out_specs=pl.BlockSpec((tm, tn), lambda i, j: (i, j)),
        out_shape=jax.ShapeDtypeStruct((m, n), F32),
        scratch_shapes=scratch,
        compiler_params=_params("arbitrary", "arbitrary"),
        name="mm_res",
    )(a, w, resid)


def _proj_body(x_ref, g_ref, w_ref, *rest, mode, chunk):
    o_ref = rest[-1]
    xn = _rms(x_ref[...], g_ref[...]).astype(BF16)
    for c0 in range(0, o_ref.shape[-1], chunk):
        y = _dot(xn, w_ref[:, c0:c0 + chunk])
        if mode in ("q", "k"):
            c, sa, sb = rest[0][...], rest[1][...], rest[2][...]
            parts = []
            for l0 in range(0, chunk, LANES):
                yc = y[:, l0:l0 + LANES]
                parts.append(yc * c + pltpu.roll(yc, LANES - ROT_DIM // 2, 1) * sa
                             + pltpu.roll(yc, ROT_DIM // 2, 1) * sb)
            y = jnp.concatenate(parts, axis=1)
            if mode == "q":
                y = y * (HEAD_DIM ** -0.5)
        elif mode in ("bias", "gelu"):
            y = y + rest[0][:, c0:c0 + chunk]
            if mode == "gelu":
                y = jax.nn.gelu(y)
        o_ref[:, c0:c0 + chunk] = y.astype(o_ref.dtype)


def _proj(x, g, w, layer, mode, *, tabs=None, period_rows=None, bias=None):
    m = x.shape[0]
    n = w.shape[-1]
    tm = _row_tile(m, 1024 if n <= D_MODEL else 512)
    chunk = 512 if n % 512 == 0 else n // 3
    extra, extra_specs = [], []
    if mode in ("q", "k"):
        pb = period_rows // tm
        extra = list(tabs)
        extra_specs = [pl.BlockSpec((tm, LANES), lambda i: (i % pb, 0))] * 3
    elif mode in ("bias", "gelu"):
        extra = [bias]
        extra_specs = [pl.BlockSpec((None, 1, n), lambda i: (layer, 0, 0))]
    body = functools.partial(_proj_body, mode=mode, chunk=chunk)
    return pl.pallas_call(
        body,
        grid=(m // tm,),
        in_specs=[
            pl.BlockSpec((tm, D_MODEL), lambda i: (i, 0)),
            pl.BlockSpec((1, D_MODEL), lambda i: (0, 0)),
            pl.BlockSpec((None, D_MODEL, n), lambda i: (layer, 0, 0), pipeline_mode=pl.Buffered(1)),
            *extra_specs,
        ],
        out_specs=pl.BlockSpec((tm, n), lambda i: (i, 0)),
        out_shape=jax.ShapeDtypeStruct((m, n), BF16 if mode == "q" else F32),
        compiler_params=_params("arbitrary"),
        name="proj_" + mode,
    )(x, g, w, *extra)


def _lam_from(lp):
    s1 = jnp.sum(lp[0:1, :] * lp[1:2, :], axis=-1, keepdims=True)
    s2 = jnp.sum(lp[2:3, :] * lp[3:4, :], axis=-1, keepdims=True)
    return jnp.exp(s1) - jnp.exp(s2)


def _pattn_body(lp_ref, g_ref, q_ref, k_ref, v_ref, o_ref, kb_ref, vb_ref, *, tq, lam_init):
    qi = pl.program_id(2)

    @pl.when(qi == 0)
    def _():
        kb_ref[...] = k_ref[...].astype(BF16)
        vb_ref[...] = v_ref[...].astype(BF16)

    q = q_ref[...]
    lane = lax.broadcasted_iota(jnp.int32, (tq, V_DIM), 1)
    zero = jnp.zeros_like(q)
    qs = jnp.concatenate([jnp.where(lane < HEAD_DIM, q, zero),
                          jnp.where(lane >= HEAD_DIM, q, zero)], axis=0)

    def scores(j):
        kt = kb_ref[pl.ds(pl.multiple_of(j * tq, tq), tq), :]
        return _dot_nt(qs, kt)

    def values(j):
        return vb_ref[pl.ds(pl.multiple_of(j * tq, tq), tq), :]

    row = lax.broadcasted_iota(jnp.int32, (2 * tq, tq), 0)
    col = lax.broadcasted_iota(jnp.int32, (2 * tq, tq), 1)
    qrow = jnp.where(row >= tq, row - tq, row)
    s = jnp.where(col <= qrow, scores(qi), NEG_INF)
    m0 = jnp.max(s, axis=-1, keepdims=True)
    p = jnp.exp(s - m0)
    l0 = jnp.sum(p, axis=-1, keepdims=True)
    acc0 = _dot(p.astype(BF16), values(qi))

    def body(j, carry):
        m, l, acc = carry
        s = scores(j)
        m_new = jnp.maximum(m, jnp.max(s, axis=-1, keepdims=True))
        alpha = jnp.exp(m - m_new)
        p = jnp.exp(s - m_new)
        l = alpha * l + jnp.sum(p, axis=-1, keepdims=True)
        acc = alpha * acc + _dot(p.astype(BF16), values(j))
        return m_new, l, acc

    _, l, acc = lax.fori_loop(0, qi, body, (m0, l0, acc0))
    o = acc / l
    lam = _lam_from(lp_ref[...]) + lam_init
    o = o[:tq] - lam * o[tq:]
    o_ref[...] = (_rms(o, g_ref[...]) * (1.0 - lam_init)).astype(BF16)


def _prompt_attention(q, k, v, lp, g, batch, seq, lam_init):
    tq = 512
    nq = seq // tq
    body = functools.partial(_pattn_body, tq=tq, lam_init=lam_init)
    return pl.pallas_call(
        body,
        grid=(batch, N_HEADS, nq),
        in_specs=[
            pl.BlockSpec((4, HEAD_DIM), lambda b, h, i: (0, 0)),
            pl.BlockSpec((1, V_DIM), lambda b, h, i: (0, 0)),
            pl.BlockSpec((tq, V_DIM), lambda b, h, i: (b * nq + i, h)),
            pl.BlockSpec((seq, V_DIM), lambda b, h, i: (b, h)),
            pl.BlockSpec((seq, V_DIM), lambda b, h, i: (b, h)),
        ],
        out_specs=pl.BlockSpec((tq, V_DIM), lambda b, h, i: (b * nq + i, h)),
        out_shape=jax.ShapeDtypeStruct((batch * seq, ATTN_WIDTH), BF16),
        scratch_shapes=[pltpu.VMEM((seq, V_DIM), BF16), pltpu.VMEM((seq, V_DIM), BF16)],
        compiler_params=_params("arbitrary", "arbitrary", "arbitrary"),
        name="prompt_attn",
    )(lp, g, q, k, v)


HEAD_GROUP = SUBLANES
N_GROUPS = N_HEADS // HEAD_GROUP
PAGES_PER_STEP = 4


def _dattn_body(pt_ref, lp_ref, g_ref, q_ref, kn_ref, vn_ref, *rest,
                n_steps, page, dec_seq, lam_init):
    del pt_ref
    kp_refs = rest[:PAGES_PER_STEP]
    vp_refs = rest[PAGES_PER_STEP:2 * PAGES_PER_STEP]
    o_ref, m_ref, l_ref, acc_ref, bias_ref = rest[2 * PAGES_PER_STEP:]
    step = pl.program_id(1)
    rows = 2 * dec_seq * HEAD_GROUP

    def group(ref, gi):
        x = ref[:, gi * HEAD_GROUP:(gi + 1) * HEAD_GROUP, :]
        return x.reshape(x.shape[0] * HEAD_GROUP, V_DIM).astype(BF16)

    def qg(gi):
        return q_ref[gi * rows:(gi + 1) * rows, :]

    @pl.when(step == 0)
    def _():
        r = lax.broadcasted_iota(jnp.int32, (rows, page * HEAD_GROUP), 0)
        c = lax.broadcasted_iota(jnp.int32, (rows, page * HEAD_GROUP), 1)
        same_head = (r & (HEAD_GROUP - 1)) == (c & (HEAD_GROUP - 1))
        bias_ref[...] = jnp.where(same_head, 0.0, NEG_INF)
        nk = dec_seq * HEAD_GROUP
        r2 = lax.broadcasted_iota(jnp.int32, (rows, nk), 0)
        c2 = lax.broadcasted_iota(jnp.int32, (rows, nk), 1)
        ok = ((r2 & (HEAD_GROUP - 1)) == (c2 & (HEAD_GROUP - 1))) & (
            lax.shift_right_logical(c2, 3) <= lax.shift_right_logical(r2, 4))
        for gi in range(N_GROUPS):
            sl = slice(gi * rows, (gi + 1) * rows)
            s = jnp.where(ok, _dot_nt(qg(gi), group(kn_ref, gi)), NEG_INF)
            m = jnp.max(s, axis=-1, keepdims=True)
            pr = jnp.exp(s - m)
            m_ref[sl, :] = m
            l_ref[sl, :] = jnp.sum(pr, axis=-1, keepdims=True)
            acc_ref[sl, :] = _dot(pr.astype(BF16), group(vn_ref, gi))

    bias = bias_ref[...]
    for gi in range(N_GROUPS):
        sl = slice(gi * rows, (gi + 1) * rows)
        q = qg(gi)
        ss = [_dot_nt(q, group(kp, gi)) + bias for kp in kp_refs]
        m_old = m_ref[sl, :]
        m_new = m_old
        for s in ss:
            m_new = jnp.maximum(m_new, jnp.max(s, axis=-1, keepdims=True))
        alpha = jnp.exp(m_old - m_new)
        l_new = alpha * l_ref[sl, :]
        acc = alpha * acc_ref[sl, :]
        for s, vp in zip(ss, vp_refs):
            pr = jnp.exp(s - m_new)
            l_new = l_new + jnp.sum(pr, axis=-1, keepdims=True)
            acc = acc + _dot(pr.astype(BF16), group(vp, gi))
        m_ref[sl, :] = m_new
        l_ref[sl, :] = l_new
        acc_ref[sl, :] = acc

    @pl.when(step == n_steps - 1)
    def _():
        o = acc_ref[...] / l_ref[...]
        lam = _lam_from(lp_ref[...]) + lam_init
        g = g_ref[...]
        for gi in range(N_GROUPS):
            for t in range(dec_seq):
                base = gi * rows + 2 * t * HEAD_GROUP
                o1 = o[base:base + HEAD_GROUP]
                o2 = o[base + HEAD_GROUP:base + 2 * HEAD_GROUP]
                o_ref[t, gi * HEAD_GROUP:(gi + 1) * HEAD_GROUP, :] = (
                    _rms(o1 - lam * o2, g) * (1.0 - lam_init))


def _sample_attention(qall, k_new, v_new, cache_k, cache_v, slot, page_table, lp, g, lam_init):
    db, n_pages = page_table.shape
    page = cache_k.shape[2]
    dec_seq = k_new.shape[1]
    nq = 2 * dec_seq * N_HEADS
    n_steps = n_pages // PAGES_PER_STEP
    body = functools.partial(_dattn_body, n_steps=n_steps, page=page, dec_seq=dec_seq,
                             lam_init=lam_init)

    def page_spec(i):
        return pl.BlockSpec(
            (None, None, page, N_HEADS, V_DIM),
            lambda b, s, pt: (slot, pt[b * n_pages + s * PAGES_PER_STEP + i], 0, 0, 0))

    page_specs = [page_spec(i) for i in range(PAGES_PER_STEP)]
    new_spec = pl.BlockSpec((None, dec_seq, N_HEADS, V_DIM), lambda b, s, pt: (b, 0, 0, 0))
    grid_spec = pltpu.PrefetchScalarGridSpec(
        num_scalar_prefetch=1,
        grid=(db, n_steps),
        in_specs=[
            pl.BlockSpec((4, HEAD_DIM), lambda b, s, pt: (0, 0)),
            pl.BlockSpec((1, V_DIM), lambda b, s, pt: (0, 0)),
            pl.BlockSpec((None, nq, V_DIM), lambda b, s, pt: (b, 0, 0)),
            new_spec, new_spec, *page_specs, *page_specs,
        ],
        out_specs=pl.BlockSpec((None, dec_seq, N_HEADS, V_DIM), lambda b, s, pt: (b, 0, 0, 0)),
        scratch_shapes=[
            pltpu.VMEM((nq, 1), F32), pltpu.VMEM((nq, 1), F32), pltpu.VMEM((nq, V_DIM), F32),
            pltpu.VMEM((nq // N_GROUPS, page * HEAD_GROUP), F32),
        ],
    )
    return pl.pallas_call(
        body,
        grid_spec=grid_spec,
        out_shape=jax.ShapeDtypeStruct((db, dec_seq, N_HEADS, V_DIM), F32),
        compiler_params=_params("arbitrary", "arbitrary"),
        name="sample_attn",
    )(page_table.reshape(-1), lp, g, qall, k_new, v_new,
      *([cache_k] * PAGES_PER_STEP), *([cache_v] * PAGES_PER_STEP))


def _conv_body(u_ref, buf_ref, w_ref, b_ref, uc_ref, nb_ref, pad_ref, *, t_len):
    lo = SUBLANES - (CONV_W - 1)
    pad_ref[lo:SUBLANES, :] = buf_ref[...]
    pad_ref[SUBLANES:SUBLANES + t_len, :] = u_ref[...]
    out = b_ref[...] + pad_ref[lo:lo + t_len, :] * w_ref[0:1, :]
    for j in range(1, CONV_W):
        out = out + pad_ref[lo + j:lo + j + t_len, :] * w_ref[j:j + 1, :]
    uc_ref[...] = out
    nb_ref[...] = pad_ref[t_len + lo:t_len + SUBLANES, :]


def _conv(u, buf, w, b, layer):
    bsz, t_len, _ = u.shape
    tc = 384
    body = functools.partial(_conv_body, t_len=t_len)
    return pl.pallas_call(
        body,
        grid=(bsz, D_RNN // tc),
        in_specs=[
            pl.BlockSpec((None, t_len, tc), lambda b_, c: (b_, 0, c)),
            pl.BlockSpec((None, CONV_W - 1, tc), lambda b_, c: (b_, 0, c)),
            pl.BlockSpec((None, CONV_W, tc), lambda b_, c: (layer, 0, c)),
            pl.BlockSpec((None, 1, tc), lambda b_, c: (layer, 0, c)),
        ],
        out_specs=[
            pl.BlockSpec((None, t_len, tc), lambda b_, c: (b_, 0, c)),
            pl.BlockSpec((None, CONV_W - 1, tc), lambda b_, c: (b_, 0, c)),
        ],
        out_shape=[jax.ShapeDtypeStruct(u.shape, F32), jax.ShapeDtypeStruct(buf.shape, F32)],
        scratch_shapes=[pltpu.VMEM((t_len + SUBLANES, tc), F32)],
        compiler_params=_params("arbitrary", "arbitrary"),
        name="conv",
    )(u, buf, w, b)


def _softplus(z):
    return jnp.maximum(z, 0.0) + jnp.log1p(jnp.exp(-jnp.abs(z)))


GATE_TN = 3 * LANES
GATE_TK = 7 * LANES


def _gate_window(j):
    first_block = (GATE_TN * j) // RG_BLOCK
    k0 = (RG_BLOCK * first_block) // LANES * LANES
    smallest = min if isinstance(j, int) else jnp.minimum
    return smallest(k0, D_RNN - GATE_TK)


def _banded(w):
    n_l, n_h, bi, bj = w.shape
    w = w.astype(BF16)
    dense = jnp.zeros((n_l, n_h * bi, n_h * bj), BF16)
    for h in range(n_h):
        dense = lax.dynamic_update_slice(dense, w[:, h], (0, h * bi, h * bj))
    tiles = []
    for j in range(D_RNN // GATE_TN):
        k0 = _gate_window(j)
        lo_block = (GATE_TN * j) // RG_BLOCK
        hi_block = (GATE_TN * (j + 1) - 1) // RG_BLOCK
        assert k0 <= RG_BLOCK * lo_block and RG_BLOCK * (hi_block + 1) <= k0 + GATE_TK
        tiles.append(dense[:, k0:k0 + GATE_TK, GATE_TN * j:GATE_TN * (j + 1)])
    return jnp.stack(tiles, axis=1)


def _gates_body(uc_ref, wa_ref, wx_ref, ba_ref, bx_ref, lam_ref, a_ref, xin_ref, ub_ref):
    j = pl.program_id(1)

    @pl.when(j == 0)
    def _():
        ub_ref[...] = uc_ref[...].astype(BF16)

    ub = ub_ref[:, pl.ds(pl.multiple_of(_gate_window(j), LANES), GATE_TK)]
    r = jax.nn.sigmoid(_dot(ub, wa_ref[...]) + ba_ref[...])
    i = jax.nn.sigmoid(_dot(ub, wx_ref[...]) + bx_ref[...])
    log_a = -RG_C * r * _softplus(-lam_ref[...])
    a = jnp.exp(log_a)
    u = uc_ref[:, pl.ds(pl.multiple_of(j * GATE_TN, LANES), GATE_TN)]
    a_ref[...] = a
    xin_ref[...] = jnp.sqrt(1.0 - a * a) * i * u


def _gates(uc, wa_b, wx_b, ba, bx, lam, layer):
    m = uc.shape[0]
    tm = _row_tile(m, 1024)
    vec = pl.BlockSpec((None, 1, GATE_TN), lambda i, j: (layer, 0, j))
    wsp = pl.BlockSpec((None, None, GATE_TK, GATE_TN), lambda i, j: (layer, j, 0, 0))
    osp = pl.BlockSpec((tm, GATE_TN), lambda i, j: (i, j))
    return pl.pallas_call(
        _gates_body,
        grid=(m // tm, D_RNN // GATE_TN),
        in_specs=[pl.BlockSpec((tm, D_RNN), lambda i, j: (i, 0)), wsp, wsp, vec, vec, vec],
        out_specs=[osp, osp],
        out_shape=[jax.ShapeDtypeStruct((m, D_RNN), F32), jax.ShapeDtypeStruct((m, D_RNN), F32)],
        scratch_shapes=[pltpu.VMEM((tm, D_RNN), BF16)],
        compiler_params=_params("arbitrary", "arbitrary"),
        name="gates",
    )(uc, wa_b, wx_b, ba, bx, lam)


def _scan_body(a_ref, x_ref, gate_ref, h0_ref, y_ref, ht_ref, *, t_len):
    tc = a_ref.shape[-1]
    h = h0_ref[...]
    if t_len % SUBLANES == 0:
        row = lax.broadcasted_iota(jnp.int32, (SUBLANES, tc), 0)

        def body(c, h):
            sl = pl.ds(pl.multiple_of(c * SUBLANES, SUBLANES), SUBLANES)
            a = a_ref[sl, :]
            x = x_ref[sl, :]
            d = 1
            while d < SUBLANES:
                valid = row >= d
                x = jnp.where(valid, a * pltpu.roll(x, d, 0) + x, x)
                a = jnp.where(valid, a * pltpu.roll(a, d, 0), a)
                d *= 2
            hs = a * h + x
            y_ref[sl, :] = hs * gate_ref[sl, :]
            return hs[SUBLANES - 1:SUBLANES, :]

        h = lax.fori_loop(0, t_len // SUBLANES, body, h)
    else:
        for t in range(t_len):
            h = a_ref[t:t + 1, :] * h + x_ref[t:t + 1, :]
            y_ref[t:t + 1, :] = h * gate_ref[t:t + 1, :]
    ht_ref[...] = h


def _scan(a, xin, gate, h0):
    bsz, t_len, _ = a.shape
    tc = 384
    body = functools.partial(_scan_body, t_len=t_len)
    seq_spec = pl.BlockSpec((None, t_len, tc), lambda b_, c: (b_, 0, c))
    h_spec = pl.BlockSpec((None, 1, tc), lambda b_, c: (b_, 0, c))
    return pl.pallas_call(
        body,
        grid=(bsz, D_RNN // tc),
        in_specs=[seq_spec, seq_spec, seq_spec, h_spec],
        out_specs=[seq_spec, h_spec],
        out_shape=[jax.ShapeDtypeStruct(a.shape, F32), jax.ShapeDtypeStruct((bsz, 1, D_RNN), F32)],
        compiler_params=_params("arbitrary", "arbitrary"),
        name="scan",
    )(a, xin, gate, h0)


def _rope_tables(pos):
    half = ROT_DIM // 2
    inv = ROPE_THETA ** (-jnp.arange(0, ROT_DIM, 2, dtype=F32) / ROT_DIM)
    ang = pos.astype(F32)[:, None] * inv[None, :]
    cos, sin = jnp.cos(ang), jnp.sin(ang)
    t_len = pos.shape[0]
    ones = jnp.ones((t_len, HEAD_DIM - ROT_DIM), F32)
    zer = jnp.zeros((t_len, HEAD_DIM - ROT_DIM), F32)
    zh = jnp.zeros((t_len, half), F32)
    c = jnp.concatenate([cos, cos, ones], axis=1)
    sa = jnp.concatenate([-sin, zh, zer], axis=1)
    sb = jnp.concatenate([zh, sin, zer], axis=1)
    return tuple(jnp.concatenate([t, t], axis=1) for t in (c, sa, sb))


def _trunk(x, bsz, t_len, p, ffn_w, tabs, period_rows, attn_fn, h0, buf0):
    ks, vs, hs, bufs, emitted = [], [], [], [], []

    def ffn(x, g, which, l, gf=None):
        if ffn_w is not None:
            return _ffn(x, g, *ffn_w[l][which], gf=gf), None
        names = ("pre_w1", "pre_w3", "pre_w2") if which == 0 else ("post_w1", "post_w3", "post_w2")
        y, *wb = _ffn(x, g, *(p[n] for n in names), gf=gf, layer=l)
        return y, tuple(wb)

    for l in range(DEPTH):
        j = l // 2
        x, wb_pre = ffn(x, p["ln_ffn_pre"][l:l + 1], 0, l)
        g_mix = p["ln_mix"][l:l + 1]
        if l % 2 == 0:
            lam_init = 0.8 - 0.6 * math.exp(-0.3 * l)
            q = _proj(x, g_mix, p["wq"], j, "q", tabs=tabs, period_rows=period_rows)
            k = _proj(x, g_mix, p["wk"], j, "k", tabs=tabs, period_rows=period_rows)
            v = _proj(x, g_mix, p["wv"], j, "v")
            o = attn_fn(j, q, k, v, p["lp"][j], p["attn_subln"][j:j + 1], lam_init)
            x = _mm_res(o, p["wo"], j, x)
            ks.append(k)
            vs.append(v)
        else:
            gate = _proj(x, g_mix, p["w_gate"], j, "gelu", bias=p["b_gate"])
            u = _proj(x, g_mix, p["w_in"], j, "bias", bias=p["b_in"])
            uc, nb = _conv(u.reshape(bsz, t_len, D_RNN), buf0[j], p["conv_w"], p["conv_b"], j)
            a, xin = _gates(uc.reshape(bsz * t_len, D_RNN), p["wa_b"], p["wx_b"],
                            p["ba"], p["bx"], p["lam"], j)
            yg, ht = _scan(a.reshape(bsz, t_len, D_RNN), xin.reshape(bsz, t_len, D_RNN),
                           gate.reshape(bsz, t_len, D_RNN), h0[j][:, None, :])
            x = _mm_res(yg.reshape(bsz * t_len, D_RNN), p["w_out"], j, x)
            hs.append(ht[:, 0, :])
            bufs.append(nb)
        gf = p["ln_final"] if l == DEPTH - 1 else None
        x, wb_post = ffn(x, p["ln_ffn_post"][l:l + 1], 1, l, gf)
        emitted.append((wb_pre, wb_post))
    return x, ks, vs, hs, bufs, emitted


def kernel(x_prompt, x_sample, cache_k, cache_v, state_h, state_conv, page_table, ln_ffn_pre, ffn_pre_w1, ffn_pre_w3, ffn_pre_w2, ln_mix, ln_ffn_post, ffn_post_w1, ffn_post_w3, ffn_post_w2, attn_wq, attn_wk, attn_wv, attn_lq1, attn_lk1, attn_lq2, attn_lk2, attn_subln, attn_wo, rec_w_gate, rec_b_gate, rec_w_in, rec_b_in, rec_conv_w, rec_conv_b, rec_wa, rec_ba, rec_wx, rec_bx, rec_lam, rec_w_out, ln_final):
    n_b = rec_lam.shape[0]
    bsz, seq = x_prompt.shape[0], x_prompt.shape[1]
    db, dec_seq = x_sample.shape[0], x_sample.shape[1]
    past = page_table.shape[1] * cache_k.shape[2]

    p = {
        "ln_ffn_pre": ln_ffn_pre, "ln_mix": ln_mix, "ln_ffn_post": ln_ffn_post,
        "ln_final": ln_final[None, :],
        "pre_w1": ffn_pre_w1, "pre_w3": ffn_pre_w3, "pre_w2": ffn_pre_w2,
        "post_w1": ffn_post_w1, "post_w3": ffn_post_w3, "post_w2": ffn_post_w2,
        "wq": attn_wq.astype(BF16), "wk": attn_wk.astype(BF16), "wv": attn_wv.astype(BF16),
        "lp": jnp.stack([attn_lq1, attn_lk1, attn_lq2, attn_lk2], axis=1).astype(F32),
        "attn_subln": attn_subln, "wo": attn_wo.astype(BF16),
        "w_gate": rec_w_gate.astype(BF16), "w_in": rec_w_in.astype(BF16),
        "b_gate": rec_b_gate[:, None, :], "b_in": rec_b_in[:, None, :],
        "conv_w": rec_conv_w, "conv_b": rec_conv_b[:, None, :],
        "wa_b": _banded(rec_wa), "wx_b": _banded(rec_wx),
        "ba": rec_ba[:, None, :], "bx": rec_bx[:, None, :], "lam": rec_lam[:, None, :],
        "w_out": rec_w_out.astype(BF16),
    }

    def sample_mix(j, q, k, v, lp, g, lam_init):
        q5 = q.reshape(db, dec_seq, N_HEADS, 2, HEAD_DIM)
        zero = jnp.zeros_like(q5[:, :, :, 0])
        q1 = jnp.concatenate([q5[:, :, :, 0], zero], axis=-1)
        q2 = jnp.concatenate([zero, q5[:, :, :, 1]], axis=-1)
        qall = jnp.stack([q1, q2], axis=2).reshape(db, dec_seq, 2, N_GROUPS, HEAD_GROUP, V_DIM)
        qall = qall.transpose(0, 3, 1, 2, 4, 5).reshape(db, dec_seq * 2 * N_HEADS, V_DIM)
        o = _sample_attention(qall, k.reshape(db, dec_seq, N_HEADS, V_DIM),
                              v.reshape(db, dec_seq, N_HEADS, V_DIM), cache_k, cache_v, j,
                              page_table, lp, g, lam_init)
        return o.reshape(db * dec_seq, ATTN_WIDTH)

    pos_s = past + jnp.tile(jnp.arange(dec_seq), db)
    y_s, k_s, v_s, h_s, c_s, ffn_w = _trunk(
        x_sample.reshape(db * dec_seq, D_MODEL), db, dec_seq, p, None,
        _rope_tables(pos_s), db * dec_seq, sample_mix, state_h, state_conv)

    def prompt_mix(j, q, k, v, lp, g, lam_init):
        return _prompt_attention(q, k, v, lp, g, bsz, seq, lam_init)

    h0_p = jnp.zeros((n_b, bsz, D_RNN), F32)
    buf0_p = jnp.zeros((n_b, bsz, CONV_W - 1, D_RNN), F32)
    y_p, k_p, v_p, h_p, c_p, _ = _trunk(
        x_prompt.reshape(bsz * seq, D_MODEL), bsz, seq, p, ffn_w,
        _rope_tables(jnp.arange(seq)), seq, prompt_mix, h0_p, buf0_p)

    def kv(xs, b_, t_):
        return jnp.stack(xs).reshape(len(xs), b_, t_, N_HEADS, V_DIM)

    return (y_p.reshape(bsz, seq, D_MODEL), y_s.reshape(db, dec_seq, D_MODEL),
            kv(k_p, bsz, seq), kv(v_p, bsz, seq), jnp.stack(h_p), jnp.stack(c_p),
            kv(k_s, db, dec_seq), kv(v_s, db, dec_seq), jnp.stack(h_s), jnp.stack(c_s))
```

```python
import functools
import math

import jax
import jax.numpy as jnp
from jax import lax
from jax.experimental import pallas as pl
from jax.experimental.pallas import tpu as pltpu

F32 = jnp.float32
BF16 = jnp.bfloat16

D_MODEL = 2048
DEPTH = 4
N_HEADS = 16
HEAD_DIM = 64
V_DIM = 2 * HEAD_DIM
ATTN_WIDTH = N_HEADS * V_DIM
ROT_DIM = HEAD_DIM // 4
ROPE_THETA = 500000.0
D_RNN = 2688
RG_HEADS = 16
RG_BLOCK = D_RNN // RG_HEADS
CONV_W = 4
RG_C = 8.0
D_FF = 5632
EPS = 1e-6

LANES = 128
SUBLANES = 8
VMEM_LIMIT_BYTES = 56 * 1024 * 1024
NEG_INF = float("-inf")


def _params(*sem):
    return pltpu.CompilerParams(dimension_semantics=sem, vmem_limit_bytes=VMEM_LIMIT_BYTES)


def _rms(x, g):
    ms = jnp.mean(x * x, axis=-1, keepdims=True)
    return x * lax.rsqrt(ms + EPS) * g


def _dot(a, b):
    return jnp.dot(a, b, preferred_element_type=F32)


def _dot_nt(a, b):
    return lax.dot_general(a, b, (((1,), (1,)), ((), ())), preferred_element_type=F32)


def _row_tile(m, cap):
    return cap if m % cap == 0 else m


def _ffn_body(x_ref, g_ref, w1_ref, w3_ref, w2_ref, gf_ref, o_ref, *rest, n_f, final_norm, emit):
    j = pl.program_id(1)
    xn_ref = rest[-1]

    @pl.when(j == 0)
    def _():
        xn_ref[...] = _rms(x_ref[...], g_ref[...]).astype(BF16)
        o_ref[...] = jnp.zeros_like(o_ref)

    w1, w3, w2 = w1_ref[...], w3_ref[...], w2_ref[...]
    if emit:
        w1, w3, w2 = w1.astype(BF16), w3.astype(BF16), w2.astype(BF16)
        rest[0][...], rest[1][...], rest[2][...] = w1, w3, w2
    xn = xn_ref[...]
    a = _dot(xn, w1)
    b = _dot(xn, w3)
    h = (jax.nn.silu(a) * b).astype(BF16)
    o_ref[...] += _dot(h, w2)

    @pl.when(j == n_f - 1)
    def _():
        y = x_ref[...] + 0.5 * o_ref[...]
        if final_norm:
            y = _rms(y, gf_ref[...])
        o_ref[...] = y


def _ffn(x, g, w1, w3, w2, gf=None, layer=None):
    m = x.shape[0]
    tm = _row_tile(m, 1024)
    tf = 512
    n_f = D_FF // tf
    emit = layer is not None
    assert not emit or m == tm
    final_norm = gf is not None
    if gf is None:
        gf = g
    body = functools.partial(_ffn_body, n_f=n_f, final_norm=final_norm, emit=emit)
    up_spec = pl.BlockSpec((D_MODEL, tf), lambda i, j: (0, j))
    down_spec = pl.BlockSpec((tf, D_MODEL), lambda i, j: (j, 0))
    out_specs = [pl.BlockSpec((tm, D_MODEL), lambda i, j: (i, 0))]
    out_shape = [jax.ShapeDtypeStruct((m, D_MODEL), F32)]
    w_specs = [up_spec, up_spec, down_spec]
    if emit:
        out_specs += w_specs
        out_shape += [jax.ShapeDtypeStruct(w.shape[1:], BF16) for w in (w1, w3, w2)]
        w_specs = [pl.BlockSpec((None, D_MODEL, tf), lambda i, j: (layer, 0, j)),
                   pl.BlockSpec((None, D_MODEL, tf), lambda i, j: (layer, 0, j)),
                   pl.BlockSpec((None, tf, D_MODEL), lambda i, j: (layer, j, 0))]
    outs = pl.pallas_call(
        body,
        grid=(m // tm, n_f),
        in_specs=[
            pl.BlockSpec((tm, D_MODEL), lambda i, j: (i, 0), pipeline_mode=pl.Buffered(1)),
            pl.BlockSpec((1, D_MODEL), lambda i, j: (0, 0)),
            *w_specs,
            pl.BlockSpec((1, D_MODEL), lambda i, j: (0, 0)),
        ],
        out_specs=out_specs,
        out_shape=out_shape,
        scratch_shapes=[pltpu.VMEM((tm, D_MODEL), BF16)],
        compiler_params=_params("arbitrary", "arbitrary"),
        name="ffn_cast" if emit else "ffn",
    )(x, g, w1, w3, w2, gf)
    return outs if emit else outs[0]


def _mm_res_body(a_ref, w_ref, r_ref, o_ref, *scratch):
    if scratch:
        ab_ref, = scratch

        @pl.when(pl.program_id(1) == 0)
        def _():
            ab_ref[...] = a_ref[...].astype(BF16)
    else:
        ab_ref = a_ref

    o_ref[...] = r_ref[...] + _dot(ab_ref[...], w_ref[...])


def _mm_res(a, w, layer, resid):
    m, k = a.shape
    n = w.shape[-1]
    tm = _row_tile(m, 1024)
    tn = 1024
    scratch = [] if a.dtype == BF16 else [pltpu.VMEM((tm, k), BF16)]
    return pl.pallas_call(
        _mm_res_body,
        grid=(m // tm, n // tn),
        in_specs=[
            pl.BlockSpec((tm, k), lambda i, j: (i, 0)),
            pl.BlockSpec((None, k, tn), lambda i, j: (layer, 0, j)),
            pl.BlockSpec((tm, tn), lambda i, j: (i, j)),
        ],
        out_specs=pl.BlockSpec((tm, tn), lambda i, j: (i, j)),
        out_shape=jax.ShapeDtypeStruct((m, n), F32),
        scratch_shapes=scratch,
        compiler_params=_params("arbitrary", "arbitrary"),
        name="mm_res",
    )(a, w, resid)


def _proj_conv_body(x_ref, g_ref, w_ref, b_ref, cw_ref, cb_ref, buf_ref, o_ref, nb_ref,
                    carry_ref, pad_ref, *, chunk, tiles_per_seq):
    i = pl.program_id(0)
    tm = o_ref.shape[0]
    lo = SUBLANES - (CONV_W - 1)

    @pl.when(i == 0)
    def _():
        carry_ref[...] = jnp.zeros_like(carry_ref)

    first = (i % tiles_per_seq) == 0
    xn = _rms(x_ref[...], g_ref[...]).astype(BF16)
    for c0 in range(0, o_ref.shape[-1], chunk):
        cols = slice(c0, c0 + chunk)
        u = _dot(xn, w_ref[:, cols]) + b_ref[:, cols]
        pad_ref[lo:SUBLANES, :] = jnp.where(first, buf_ref[:, cols], carry_ref[0:CONV_W - 1, cols])
        pad_ref[SUBLANES:SUBLANES + tm, :] = u
        out = cb_ref[:, cols] + pad_ref[lo:lo + tm, :] * cw_ref[0:1, cols]
        for j in range(1, CONV_W):
            out = out + pad_ref[lo + j:lo + j + tm, :] * cw_ref[j:j + 1, cols]
        o_ref[:, cols] = out
        last = pad_ref[tm + lo:tm + SUBLANES, :]
        carry_ref[0:CONV_W - 1, cols] = last
        nb_ref[:, cols] = last


def _proj_body(x_ref, g_ref, w_ref, *rest, mode, chunk, conv_args=None):
    if mode == "conv":
        _proj_conv_body(x_ref, g_ref, w_ref, *rest, chunk=chunk, **conv_args)
        return
    o_ref = rest[-1]
    xn = _rms(x_ref[...], g_ref[...]).astype(BF16)
    for c0 in range(0, o_ref.shape[-1], chunk):
        y = _dot(xn, w_ref[:, c0:c0 + chunk])
        if mode in ("q", "k"):
            c, sa, sb = rest[0][...], rest[1][...], rest[2][...]
            parts = []
            for l0 in range(0, chunk, LANES):
                yc = y[:, l0:l0 + LANES]
                parts.append(yc * c + pltpu.roll(yc, LANES - ROT_DIM // 2, 1) * sa
                             + pltpu.roll(yc, ROT_DIM // 2, 1) * sb)
            y = jnp.concatenate(parts, axis=1)
            if mode == "q":
                y = y * (HEAD_DIM ** -0.5)
        elif mode in ("bias", "gelu"):
            y = y + rest[0][:, c0:c0 + chunk]
            if mode == "gelu":
                y = jax.nn.gelu(y)
        o_ref[:, c0:c0 + chunk] = y.astype(o_ref.dtype)


def _proj_row_tile(m, n):
    return _row_tile(m, 1024 if n <= D_MODEL else 512)


def _proj(x, g, w, layer, mode, *, tabs=None, period_rows=None, bias=None, conv=None):
    m = x.shape[0]
    n = w.shape[-1]
    tm = _proj_row_tile(m, n)
    chunk = 512 if n % 512 == 0 else n // 3
    extra, extra_specs, scratch, conv_args = [], [], [], None
    out_specs = pl.BlockSpec((tm, n), lambda i: (i, 0))
    out_shape = jax.ShapeDtypeStruct((m, n), BF16 if mode == "q" else F32)
    vec_spec = pl.BlockSpec((None, 1, n), lambda i: (layer, 0, 0))
    if mode in ("q", "k"):
        pb = period_rows // tm
        extra = list(tabs)
        extra_specs = [pl.BlockSpec((tm, LANES), lambda i: (i % pb, 0))] * 3
    elif mode in ("bias", "gelu"):
        extra = [bias]
        extra_specs = [vec_spec]
    elif mode == "conv":
        cw, cb, buf, t_len = conv
        tiles_per_seq = t_len // tm
        assert tiles_per_seq * tm == t_len
        conv_args = dict(tiles_per_seq=tiles_per_seq)
        seq_spec = pl.BlockSpec((None, CONV_W - 1, n), lambda i: (i // tiles_per_seq, 0, 0))
        extra = [bias, cw, cb, buf]
        extra_specs = [vec_spec, pl.BlockSpec((None, CONV_W, n), lambda i: (layer, 0, 0)),
                       vec_spec, seq_spec]
        out_specs = [out_specs, seq_spec]
        out_shape = [out_shape, jax.ShapeDtypeStruct(buf.shape, F32)]
        scratch = [pltpu.VMEM((SUBLANES, n), F32), pltpu.VMEM((tm + SUBLANES, chunk), F32)]
    body = functools.partial(_proj_body, mode=mode, chunk=chunk, conv_args=conv_args)
    return pl.pallas_call(
        body,
        grid=(m // tm,),
        in_specs=[
            pl.BlockSpec((tm, D_MODEL), lambda i: (i, 0)),
            pl.BlockSpec((1, D_MODEL), lambda i: (0, 0)),
            pl.BlockSpec((None, D_MODEL, n), lambda i: (layer, 0, 0), pipeline_mode=pl.Buffered(1)),
            *extra_specs,
        ],
        out_specs=out_specs,
        out_shape=out_shape,
        scratch_shapes=scratch,
        compiler_params=_params("arbitrary"),
        name="proj_" + mode,
    )(x, g, w, *extra)


def _lam_from(lp):
    s1 = jnp.sum(lp[0:1, :] * lp[1:2, :], axis=-1, keepdims=True)
    s2 = jnp.sum(lp[2:3, :] * lp[3:4, :], axis=-1, keepdims=True)
    return jnp.exp(s1) - jnp.exp(s2)


def _pattn_body(lp_ref, g_ref, q_ref, k_ref, v_ref, o_ref, kb_ref, vb_ref, *, tq, lam_init):
    qi = pl.program_id(2)

    @pl.when(qi == 0)
    def _():
        kb_ref[...] = k_ref[...].astype(BF16)
        vb_ref[...] = v_ref[...].astype(BF16)

    q = q_ref[...]
    lane = lax.broadcasted_iota(jnp.int32, (tq, V_DIM), 1)
    zero = jnp.zeros_like(q)
    qs = jnp.concatenate([jnp.where(lane < HEAD_DIM, q, zero),
                          jnp.where(lane >= HEAD_DIM, q, zero)], axis=0)

    def scores(j):
        kt = kb_ref[pl.ds(pl.multiple_of(j * tq, tq), tq), :]
        return _dot_nt(qs, kt)

    def values(j):
        return vb_ref[pl.ds(pl.multiple_of(j * tq, tq), tq), :]

    row = lax.broadcasted_iota(jnp.int32, (2 * tq, tq), 0)
    col = lax.broadcasted_iota(jnp.int32, (2 * tq, tq), 1)
    qrow = jnp.where(row >= tq, row - tq, row)
    s = jnp.where(col <= qrow, scores(qi), NEG_INF)
    m0 = jnp.max(s, axis=-1, keepdims=True)
    p = jnp.exp(s - m0)
    l0 = jnp.sum(p, axis=-1, keepdims=True)
    acc0 = _dot(p.astype(BF16), values(qi))

    def body(j, carry):
        m, l, acc = carry
        s = scores(j)
        m_new = jnp.maximum(m, jnp.max(s, axis=-1, keepdims=True))
        alpha = jnp.exp(m - m_new)
        p = jnp.exp(s - m_new)
        l = alpha * l + jnp.sum(p, axis=-1, keepdims=True)
        acc = alpha * acc + _dot(p.astype(BF16), values(j))
        return m_new, l, acc

    _, l, acc = lax.fori_loop(0, qi, body, (m0, l0, acc0))
    o = acc / l
    lam = _lam_from(lp_ref[...]) + lam_init
    o = o[:tq] - lam * o[tq:]
    o_ref[...] = (_rms(o, g_ref[...]) * (1.0 - lam_init)).astype(BF16)


def _prompt_attention(q, k, v, lp, g, batch, seq, lam_init):
    tq = 512
    nq = seq // tq
    body = functools.partial(_pattn_body, tq=tq, lam_init=lam_init)
    return pl.pallas_call(
        body,
        grid=(batch, N_HEADS, nq),
        in_specs=[
            pl.BlockSpec((4, HEAD_DIM), lambda b, h, i: (0, 0)),
            pl.BlockSpec((1, V_DIM), lambda b, h, i: (0, 0)),
            pl.BlockSpec((tq, V_DIM), lambda b, h, i: (b * nq + i, h)),
            pl.BlockSpec((seq, V_DIM), lambda b, h, i: (b, h)),
            pl.BlockSpec((seq, V_DIM), lambda b, h, i: (b, h)),
        ],
        out_specs=pl.BlockSpec((tq, V_DIM), lambda b, h, i: (b * nq + i, h)),
        out_shape=jax.ShapeDtypeStruct((batch * seq, ATTN_WIDTH), BF16),
        scratch_shapes=[pltpu.VMEM((seq, V_DIM), BF16), pltpu.VMEM((seq, V_DIM), BF16)],
        compiler_params=_params("arbitrary", "arbitrary", "arbitrary"),
        name="prompt_attn",
    )(lp, g, q, k, v)


HEAD_GROUP = SUBLANES
N_GROUPS = N_HEADS // HEAD_GROUP
PAGES_PER_STEP = 4


def _dattn_body(pt_ref, lp_ref, g_ref, q_ref, kn_ref, vn_ref, *rest,
                n_steps, page, dec_seq, lam_init):
    del pt_ref
    kp_refs = rest[:PAGES_PER_STEP]
    vp_refs = rest[PAGES_PER_STEP:2 * PAGES_PER_STEP]
    o_ref, m_ref, l_ref, acc_ref, bias_ref = rest[2 * PAGES_PER_STEP:]
    step = pl.program_id(1)
    rows = 2 * dec_seq * HEAD_GROUP

    def group(ref, gi):
        x = ref[:, gi * HEAD_GROUP:(gi + 1) * HEAD_GROUP, :]
        return x.reshape(x.shape[0] * HEAD_GROUP, V_DIM).astype(BF16)

    def qg(gi):
        return q_ref[gi * rows:(gi + 1) * rows, :]

    @pl.when(step == 0)
    def _():
        r = lax.broadcasted_iota(jnp.int32, (rows, page * HEAD_GROUP), 0)
        c = lax.broadcasted_iota(jnp.int32, (rows, page * HEAD_GROUP), 1)
        same_head = (r & (HEAD_GROUP - 1)) == (c & (HEAD_GROUP - 1))
        bias_ref[...] = jnp.where(same_head, 0.0, NEG_INF)
        nk = dec_seq * HEAD_GROUP
        r2 = lax.broadcasted_iota(jnp.int32, (rows, nk), 0)
        c2 = lax.broadcasted_iota(jnp.int32, (rows, nk), 1)
        ok = ((r2 & (HEAD_GROUP - 1)) == (c2 & (HEAD_GROUP - 1))) & (
            lax.shift_right_logical(c2, 3) <= lax.shift_right_logical(r2, 4))
        for gi in range(N_GROUPS):
            sl = slice(gi * rows, (gi + 1) * rows)
            s = jnp.where(ok, _dot_nt(qg(gi), group(kn_ref, gi)), NEG_INF)
            m = jnp.max(s, axis=-1, keepdims=True)
            pr = jnp.exp(s - m)
            m_ref[sl, :] = m
            l_ref[sl, :] = jnp.sum(pr, axis=-1, keepdims=True)
            acc_ref[sl, :] = _dot(pr.astype(BF16), group(vn_ref, gi))

    bias = bias_ref[...]
    for gi in range(N_GROUPS):
        sl = slice(gi * rows, (gi + 1) * rows)
        q = qg(gi)
        ss = [_dot_nt(q, group(kp, gi)) + bias for kp in kp_refs]
        m_old = m_ref[sl, :]
        m_new = m_old
        for s in ss:
            m_new = jnp.maximum(m_new, jnp.max(s, axis=-1, keepdims=True))
        alpha = jnp.exp(m_old - m_new)
        l_new = alpha * l_ref[sl, :]
        acc = alpha * acc_ref[sl, :]
        for s, vp in zip(ss, vp_refs):
            pr = jnp.exp(s - m_new)
            l_new = l_new + jnp.sum(pr, axis=-1, keepdims=True)
            acc = acc + _dot(pr.astype(BF16), group(vp, gi))
        m_ref[sl, :] = m_new
        l_ref[sl, :] = l_new
        acc_ref[sl, :] = acc

    @pl.when(step == n_steps - 1)
    def _():
        o = acc_ref[...] / l_ref[...]
        lam = _lam_from(lp_ref[...]) + lam_init
        g = g_ref[...]
        for gi in range(N_GROUPS):
            for t in range(dec_seq):
                base = gi * rows + 2 * t * HEAD_GROUP
                o1 = o[base:base + HEAD_GROUP]
                o2 = o[base + HEAD_GROUP:base + 2 * HEAD_GROUP]
                o_ref[t, gi * HEAD_GROUP:(gi + 1) * HEAD_GROUP, :] = (
                    _rms(o1 - lam * o2, g) * (1.0 - lam_init))


def _sample_attention(qall, k_new, v_new, cache_k, cache_v, slot, page_table, lp, g, lam_init):
    db, n_pages = page_table.shape
    page = cache_k.shape[2]
    dec_seq = k_new.shape[1]
    nq = 2 * dec_seq * N_HEADS
    n_steps = n_pages // PAGES_PER_STEP
    body = functools.partial(_dattn_body, n_steps=n_steps, page=page, dec_seq=dec_seq,
                             lam_init=lam_init)

    def page_spec(i):
        return pl.BlockSpec(
            (None, None, page, N_HEADS, V_DIM),
            lambda b, s, pt: (slot, pt[b * n_pages + s * PAGES_PER_STEP + i], 0, 0, 0))

    page_specs = [page_spec(i) for i in range(PAGES_PER_STEP)]
    new_spec = pl.BlockSpec((None, dec_seq, N_HEADS, V_DIM), lambda b, s, pt: (b, 0, 0, 0))
    grid_spec = pltpu.PrefetchScalarGridSpec(
        num_scalar_prefetch=1,
        grid=(db, n_steps),
        in_specs=[
            pl.BlockSpec((4, HEAD_DIM), lambda b, s, pt: (0, 0)),
            pl.BlockSpec((1, V_DIM), lambda b, s, pt: (0, 0)),
            pl.BlockSpec((None, nq, V_DIM), lambda b, s, pt: (b, 0, 0)),
            new_spec, new_spec, *page_specs, *page_specs,
        ],
        out_specs=pl.BlockSpec((None, dec_seq, N_HEADS, V_DIM), lambda b, s, pt: (b, 0, 0, 0)),
        scratch_shapes=[
            pltpu.VMEM((nq, 1), F32), pltpu.VMEM((nq, 1), F32), pltpu.VMEM((nq, V_DIM), F32),
            pltpu.VMEM((nq // N_GROUPS, page * HEAD_GROUP), F32),
        ],
    )
    return pl.pallas_call(
        body,
        grid_spec=grid_spec,
        out_shape=jax.ShapeDtypeStruct((db, dec_seq, N_HEADS, V_DIM), F32),
        compiler_params=_params("arbitrary", "arbitrary"),
        name="sample_attn",
    )(page_table.reshape(-1), lp, g, qall, k_new, v_new,
      *([cache_k] * PAGES_PER_STEP), *([cache_v] * PAGES_PER_STEP))


def _conv_body(u_ref, buf_ref, w_ref, b_ref, uc_ref, nb_ref, pad_ref, *, t_len):
    lo = SUBLANES - (CONV_W - 1)
    pad_ref[lo:SUBLANES, :] = buf_ref[...]
    pad_ref[SUBLANES:SUBLANES + t_len, :] = u_ref[...]
    out = b_ref[...] + pad_ref[lo:lo + t_len, :] * w_ref[0:1, :]
    for j in range(1, CONV_W):
        out = out + pad_ref[lo + j:lo + j + t_len, :] * w_ref[j:j + 1, :]
    uc_ref[...] = out
    nb_ref[...] = pad_ref[t_len + lo:t_len + SUBLANES, :]


def _conv(u, buf, w, b, layer):
    bsz, t_len, _ = u.shape
    tc = 384
    body = functools.partial(_conv_body, t_len=t_len)
    return pl.pallas_call(
        body,
        grid=(bsz, D_RNN // tc),
        in_specs=[
            pl.BlockSpec((None, t_len, tc), lambda b_, c: (b_, 0, c)),
            pl.BlockSpec((None, CONV_W - 1, tc), lambda b_, c: (b_, 0, c)),
            pl.BlockSpec((None, CONV_W, tc), lambda b_, c: (layer, 0, c)),
            pl.BlockSpec((None, 1, tc), lambda b_, c: (layer, 0, c)),
        ],
        out_specs=[
            pl.BlockSpec((None, t_len, tc), lambda b_, c: (b_, 0, c)),
            pl.BlockSpec((None, CONV_W - 1, tc), lambda b_, c: (b_, 0, c)),
        ],
        out_shape=[jax.ShapeDtypeStruct(u.shape, F32), jax.ShapeDtypeStruct(buf.shape, F32)],
        scratch_shapes=[pltpu.VMEM((t_len + SUBLANES, tc), F32)],
        compiler_params=_params("arbitrary", "arbitrary"),
        name="conv",
    )(u, buf, w, b)


def _softplus(z):
    return jnp.maximum(z, 0.0) + jnp.log1p(jnp.exp(-jnp.abs(z)))


GATE_TN = 3 * LANES
GATE_TK = 7 * LANES


def _gate_window(j):
    first_block = (GATE_TN * j) // RG_BLOCK
    k0 = (RG_BLOCK * first_block) // LANES * LANES
    smallest = min if isinstance(j, int) else jnp.minimum
    return smallest(k0, D_RNN - GATE_TK)


def _banded(w):
    n_l, n_h, bi, bj = w.shape
    w = w.astype(BF16)
    dense = jnp.zeros((n_l, n_h * bi, n_h * bj), BF16)
    for h in range(n_h):
        dense = lax.dynamic_update_slice(dense, w[:, h], (0, h * bi, h * bj))
    tiles = []
    for j in range(D_RNN // GATE_TN):
        k0 = _gate_window(j)
        lo_block = (GATE_TN * j) // RG_BLOCK
        hi_block = (GATE_TN * (j + 1) - 1) // RG_BLOCK
        assert k0 <= RG_BLOCK * lo_block and RG_BLOCK * (hi_block + 1) <= k0 + GATE_TK
        tiles.append(dense[:, k0:k0 + GATE_TK, GATE_TN * j:GATE_TN * (j + 1)])
    return jnp.stack(tiles, axis=1)


def _gates_body(uc_ref, wa_ref, wx_ref, ba_ref, bx_ref, lam_ref, a_ref, xin_ref, ub_ref):
    j = pl.program_id(1)

    @pl.when(j == 0)
    def _():
        ub_ref[...] = uc_ref[...].astype(BF16)

    ub = ub_ref[:, pl.ds(pl.multiple_of(_gate_window(j), LANES), GATE_TK)]
    r = jax.nn.sigmoid(_dot(ub, wa_ref[...]) + ba_ref[...])
    i = jax.nn.sigmoid(_dot(ub, wx_ref[...]) + bx_ref[...])
    log_a = -RG_C * r * _softplus(-lam_ref[...])
    a = jnp.exp(log_a)
    u = uc_ref[:, pl.ds(pl.multiple_of(j * GATE_TN, LANES), GATE_TN)]
    a_ref[...] = a
    xin_ref[...] = jnp.sqrt(1.0 - a * a) * i * u


def _gates(uc, wa_b, wx_b, ba, bx, lam, layer):
    m = uc.shape[0]
    tm = _row_tile(m, 1024)
    vec = pl.BlockSpec((None, 1, GATE_TN), lambda i, j: (layer, 0, j))
    wsp = pl.BlockSpec((None, None, GATE_TK, GATE_TN), lambda i, j: (layer, j, 0, 0))
    osp = pl.BlockSpec((tm, GATE_TN), lambda i, j: (i, j))
    return pl.pallas_call(
        _gates_body,
        grid=(m // tm, D_RNN // GATE_TN),
        in_specs=[pl.BlockSpec((tm, D_RNN), lambda i, j: (i, 0)), wsp, wsp, vec, vec, vec],
        out_specs=[osp, osp],
        out_shape=[jax.ShapeDtypeStruct((m, D_RNN), F32), jax.ShapeDtypeStruct((m, D_RNN), F32)],
        scratch_shapes=[pltpu.VMEM((tm, D_RNN), BF16)],
        compiler_params=_params("arbitrary", "arbitrary"),
        name="gates",
    )(uc, wa_b, wx_b, ba, bx, lam)


def _scan_body(a_ref, x_ref, gate_ref, h0_ref, y_ref, ht_ref, *, t_len):
    tc = a_ref.shape[-1]
    h = h0_ref[...]
    step = 2 * SUBLANES
    if t_len % step == 0:
        row = lax.broadcasted_iota(jnp.int32, (SUBLANES, tc), 0)

        def chunk(sl, h):
            a = a_ref[sl, :]
            x = x_ref[sl, :]
            d = 1
            while d < SUBLANES:
                valid = row >= d
                x = jnp.where(valid, a * pltpu.roll(x, d, 0) + x, x)
                a = jnp.where(valid, a * pltpu.roll(a, d, 0), a)
                d *= 2
            return a * h + x

        def body(c, h):
            base = pl.multiple_of(c * step, step)
            hs1 = chunk(pl.ds(base, SUBLANES), h)
            hs2 = chunk(pl.ds(pl.multiple_of(base + SUBLANES, SUBLANES), SUBLANES),
                        hs1[SUBLANES - 1:SUBLANES, :])
            sl = pl.ds(base, step)
            y_ref[sl, :] = (jnp.concatenate([hs1, hs2], axis=0) * gate_ref[sl, :]).astype(y_ref.dtype)
            return hs2[SUBLANES - 1:SUBLANES, :]

        h = lax.fori_loop(0, t_len // step, body, h)
    else:
        for t in range(t_len):
            h = a_ref[t:t + 1, :] * h + x_ref[t:t + 1, :]
            y_ref[t:t + 1, :] = h * gate_ref[t:t + 1, :]
    ht_ref[...] = h


def _scan(a, xin, gate, h0):
    bsz, t_len, _ = a.shape
    tc = 384
    body = functools.partial(_scan_body, t_len=t_len)
    seq_spec = pl.BlockSpec((None, t_len, tc), lambda b_, c: (b_, 0, c))
    h_spec = pl.BlockSpec((None, 1, tc), lambda b_, c: (b_, 0, c))
    y_dtype = BF16 if t_len % (2 * SUBLANES) == 0 else F32
    return pl.pallas_call(
        body,
        grid=(bsz, D_RNN // tc),
        in_specs=[seq_spec, seq_spec, seq_spec, h_spec],
        out_specs=[seq_spec, h_spec],
        out_shape=[jax.ShapeDtypeStruct(a.shape, y_dtype), jax.ShapeDtypeStruct((bsz, 1, D_RNN), F32)],
        compiler_params=_params("arbitrary", "arbitrary"),
        name="scan",
    )(a, xin, gate, h0)


def _rope_tables(pos):
    half = ROT_DIM // 2
    inv = ROPE_THETA ** (-jnp.arange(0, ROT_DIM, 2, dtype=F32) / ROT_DIM)
    ang = pos.astype(F32)[:, None] * inv[None, :]
    cos, sin = jnp.cos(ang), jnp.sin(ang)
    t_len = pos.shape[0]
    ones = jnp.ones((t_len, HEAD_DIM - ROT_DIM), F32)
    zer = jnp.zeros((t_len, HEAD_DIM - ROT_DIM), F32)
    zh = jnp.zeros((t_len, half), F32)
    c = jnp.concatenate([cos, cos, ones], axis=1)
    sa = jnp.concatenate([-sin, zh, zer], axis=1)
    sb = jnp.concatenate([zh, sin, zer], axis=1)
    return tuple(jnp.concatenate([t, t], axis=1) for t in (c, sa, sb))


def _trunk(x, bsz, t_len, p, ffn_w, tabs, period_rows, attn_fn, h0, buf0):
    ks, vs, hs, bufs, emitted = [], [], [], [], []

    def ffn(x, g, which, l, gf=None):
        if ffn_w is not None:
            return _ffn(x, g, *ffn_w[l][which], gf=gf), None
        names = ("pre_w1", "pre_w3", "pre_w2") if which == 0 else ("post_w1", "post_w3", "post_w2")
        y, *wb = _ffn(x, g, *(p[n] for n in names), gf=gf, layer=l)
        return y, tuple(wb)

    for l in range(DEPTH):
        j = l // 2
        x, wb_pre = ffn(x, p["ln_ffn_pre"][l:l + 1], 0, l)
        g_mix = p["ln_mix"][l:l + 1]
        if l % 2 == 0:
            lam_init = 0.8 - 0.6 * math.exp(-0.3 * l)
            q = _proj(x, g_mix, p["wq"], j, "q", tabs=tabs, period_rows=period_rows)
            k = _proj(x, g_mix, p["wk"], j, "k", tabs=tabs, period_rows=period_rows)
            v = _proj(x, g_mix, p["wv"], j, "v")
            o = attn_fn(j, q, k, v, p["lp"][j], p["attn_subln"][j:j + 1], lam_init)
            x = _mm_res(o, p["wo"], j, x)
            ks.append(k)
            vs.append(v)
        else:
            gate = _proj(x, g_mix, p["w_gate"], j, "gelu", bias=p["b_gate"])
            if t_len % _proj_row_tile(bsz * t_len, D_RNN) == 0:
                uc, nb = _proj(x, g_mix, p["w_in"], j, "conv", bias=p["b_in"],
                               conv=(p["conv_w"], p["conv_b"], buf0[j], t_len))
            else:
                u = _proj(x, g_mix, p["w_in"], j, "bias", bias=p["b_in"])
                uc, nb = _conv(u.reshape(bsz, t_len, D_RNN), buf0[j], p["conv_w"], p["conv_b"], j)
            a, xin = _gates(uc.reshape(bsz * t_len, D_RNN), p["wa_b"], p["wx_b"],
                            p["ba"], p["bx"], p["lam"], j)
            yg, ht = _scan(a.reshape(bsz, t_len, D_RNN), xin.reshape(bsz, t_len, D_RNN),
                           gate.reshape(bsz, t_len, D_RNN), h0[j][:, None, :])
            x = _mm_res(yg.reshape(bsz * t_len, D_RNN), p["w_out"], j, x)
            hs.append(ht[:, 0, :])
            bufs.append(nb)
        gf = p["ln_final"] if l == DEPTH - 1 else None
        x, wb_post = ffn(x, p["ln_ffn_post"][l:l + 1], 1, l, gf)
        emitted.append((wb_pre, wb_post))
    return x, ks, vs, hs, bufs, emitted


def kernel(x_prompt, x_sample, cache_k, cache_v, state_h, state_conv, page_table, ln_ffn_pre, ffn_pre_w1, ffn_pre_w3, ffn_pre_w2, ln_mix, ln_ffn_post, ffn_post_w1, ffn_post_w3, ffn_post_w2, attn_wq, attn_wk, attn_wv, attn_lq1, attn_lk1, attn_lq2, attn_lk2, attn_subln, attn_wo, rec_w_gate, rec_b_gate, rec_w_in, rec_b_in, rec_conv_w, rec_conv_b, rec_wa, rec_ba, rec_wx, rec_bx, rec_lam, rec_w_out, ln_final):
    n_b = rec_lam.shape[0]
    bsz, seq = x_prompt.shape[0], x_prompt.shape[1]
    db, dec_seq = x_sample.shape[0], x_sample.shape[1]
    past = page_table.shape[1] * cache_k.shape[2]

    p = {
        "ln_ffn_pre": ln_ffn_pre, "ln_mix": ln_mix, "ln_ffn_post": ln_ffn_post,
        "ln_final": ln_final[None, :],
        "pre_w1": ffn_pre_w1, "pre_w3": ffn_pre_w3, "pre_w2": ffn_pre_w2,
        "post_w1": ffn_post_w1, "post_w3": ffn_post_w3, "post_w2": ffn_post_w2,
        "wq": attn_wq.astype(BF16), "wk": attn_wk.astype(BF16), "wv": attn_wv.astype(BF16),
        "lp": jnp.stack([attn_lq1, attn_lk1, attn_lq2, attn_lk2], axis=1).astype(F32),
        "attn_subln": attn_subln, "wo": attn_wo.astype(BF16),
        "w_gate": rec_w_gate.astype(BF16), "w_in": rec_w_in.astype(BF16),
        "b_gate": rec_b_gate[:, None, :], "b_in": rec_b_in[:, None, :],
        "conv_w": rec_conv_w, "conv_b": rec_conv_b[:, None, :],
        "wa_b": _banded(rec_wa), "wx_b": _banded(rec_wx),
        "ba": rec_ba[:, None, :], "bx": rec_bx[:, None, :], "lam": rec_lam[:, None, :],
        "w_out": rec_w_out.astype(BF16),
    }

    def sample_mix(j, q, k, v, lp, g, lam_init):
        q5 = q.reshape(db, dec_seq, N_HEADS, 2, HEAD_DIM)
        zero = jnp.zeros_like(q5[:, :, :, 0])
        q1 = jnp.concatenate([q5[:, :, :, 0], zero], axis=-1)
        q2 = jnp.concatenate([zero, q5[:, :, :, 1]], axis=-1)
        qall = jnp.stack([q1, q2], axis=2).reshape(db, dec_seq, 2, N_GROUPS, HEAD_GROUP, V_DIM)
        qall = qall.transpose(0, 3, 1, 2, 4, 5).reshape(db, dec_seq * 2 * N_HEADS, V_DIM)
        o = _sample_attention(qall, k.reshape(db, dec_seq, N_HEADS, V_DIM),
                              v.reshape(db, dec_seq, N_HEADS, V_DIM), cache_k, cache_v, j,
                              page_table, lp, g, lam_init)
        return o.reshape(db * dec_seq, ATTN_WIDTH)

    pos_s = past + jnp.tile(jnp.arange(dec_seq), db)
    y_s, k_s, v_s, h_s, c_s, ffn_w = _trunk(
        x_sample.reshape(db * dec_seq, D_MODEL), db, dec_seq, p, None,
        _rope_tables(pos_s), db * dec_seq, sample_mix, state_h, state_conv)

    def prompt_mix(j, q, k, v, lp, g, lam_init):
        return _prompt_attention(q, k, v, lp, g, bsz, seq, lam_init)

    h0_p = jnp.zeros((n_b, bsz, D_RNN), F32)
    buf0_p = jnp.zeros((n_b, bsz, CONV_W - 1, D_RNN), F32)
    y_p, k_p, v_p, h_p, c_p, _ = _trunk(
        x_prompt.reshape(bsz * seq, D_MODEL), bsz, seq, p, ffn_w,
        _rope_tables(jnp.arange(seq)), seq, prompt_mix, h0_p, buf0_p)

    def kv(xs, b_, t_):
        return jnp.stack(xs).reshape(len(xs), b_, t_, N_HEADS, V_DIM)

    return (y_p.reshape(bsz, seq, D_MODEL), y_s.reshape(db, dec_seq, D_MODEL),
            kv(k_p, bsz, seq), kv(v_p, bsz, seq), jnp.stack(h_p), jnp.stack(c_p),
            kv(k_s, db, dec_seq), kv(v_s, db, dec_seq), jnp.stack(h_s), jnp.stack(c_s))
```

```python
import functools
import math

import jax
import jax.numpy as jnp
from jax import lax
from jax.experimental import pallas as pl
from jax.experimental.pallas import tpu as pltpu

F32 = jnp.float32
BF16 = jnp.bfloat16

D_MODEL = 2048
DEPTH = 4
N_HEADS = 16
HEAD_DIM = 64
V_DIM = 2 * HEAD_DIM
ATTN_WIDTH = N_HEADS * V_DIM
ROT_DIM = HEAD_DIM // 4
ROPE_THETA = 500000.0
D_RNN = 2688
RG_HEADS = 16
RG_BLOCK = D_RNN // RG_HEADS
CONV_W = 4
RG_C = 8.0
D_FF = 5632
EPS = 1e-6

LANES = 128
SUBLANES = 8
VMEM_LIMIT_BYTES = 56 * 1024 * 1024
NEG_INF = float("-inf")


def _params(*sem):
    return pltpu.CompilerParams(dimension_semantics=sem, vmem_limit_bytes=VMEM_LIMIT_BYTES)


def _rms(x, g):
    ms = jnp.mean(x * x, axis=-1, keepdims=True)
    return x * lax.rsqrt(ms + EPS) * g


def _dot(a, b):
    return jnp.dot(a, b, preferred_element_type=F32)


def _dot_nt(a, b):
    return lax.dot_general(a, b, (((1,), (1,)), ((), ())), preferred_element_type=F32)


def _row_tile(m, cap):
    return cap if m % cap == 0 else m


def _ffn_body(x_ref, g_ref, w1_ref, w3_ref, w2_ref, gf_ref, o_ref, *rest, n_f, final_norm, emit):
    j = pl.program_id(1)
    xn_ref = rest[-1]

    @pl.when(j == 0)
    def _():
        xn_ref[...] = _rms(x_ref[...], g_ref[...]).astype(BF16)
        o_ref[...] = jnp.zeros_like(o_ref)

    w1, w3, w2 = w1_ref[...], w3_ref[...], w2_ref[...]
    if emit:
        w1, w3, w2 = w1.astype(BF16), w3.astype(BF16), w2.astype(BF16)
        rest[0][...], rest[1][...], rest[2][...] = w1, w3, w2
    xn = xn_ref[...]
    a = _dot(xn, w1)
    b = _dot(xn, w3)
    h = (jax.nn.silu(a) * b).astype(BF16)
    o_ref[...] += _dot(h, w2)

    @pl.when(j == n_f - 1)
    def _():
        y = x_ref[...] + 0.5 * o_ref[...]
        if final_norm:
            y = _rms(y, gf_ref[...])
        o_ref[...] = y


def _ffn(x, g, w1, w3, w2, gf=None, layer=None):
    m = x.shape[0]
    tm = _row_tile(m, 1024)
    tf = 512
    n_f = D_FF // tf
    emit = layer is not None
    assert not emit or m == tm
    final_norm = gf is not None
    if gf is None:
        gf = g
    body = functools.partial(_ffn_body, n_f=n_f, final_norm=final_norm, emit=emit)
    up_spec = pl.BlockSpec((D_MODEL, tf), lambda i, j: (0, j))
    down_spec = pl.BlockSpec((tf, D_MODEL), lambda i, j: (j, 0))
    out_specs = [pl.BlockSpec((tm, D_MODEL), lambda i, j: (i, 0))]
    out_shape = [jax.ShapeDtypeStruct((m, D_MODEL), F32)]
    w_specs = [up_spec, up_spec, down_spec]
    if emit:
        out_specs += w_specs
        out_shape += [jax.ShapeDtypeStruct(w.shape[1:], BF16) for w in (w1, w3, w2)]
        w_specs = [pl.BlockSpec((None, D_MODEL, tf), lambda i, j: (layer, 0, j)),
                   pl.BlockSpec((None, D_MODEL, tf), lambda i, j: (layer, 0, j)),
                   pl.BlockSpec((None, tf, D_MODEL), lambda i, j: (layer, j, 0))]
    outs = pl.pallas_call(
        body,
        grid=(m // tm, n_f),
        in_specs=[
            pl.BlockSpec((tm, D_MODEL), lambda i, j: (i, 0), pipeline_mode=pl.Buffered(1)),
            pl.BlockSpec((1, D_MODEL), lambda i, j: (0, 0)),
            *w_specs,
            pl.BlockSpec((1, D_MODEL), lambda i, j: (0, 0)),
        ],
        out_specs=out_specs,
        out_shape=out_shape,
        scratch_shapes=[pltpu.VMEM((tm, D_MODEL), BF16)],
        compiler_params=_params("arbitrary", "arbitrary"),
        name="ffn_cast" if emit else "ffn",
    )(x, g, w1, w3, w2, gf)
    return outs if emit else outs[0]


def _mm_res_body(a_ref, w_ref, r_ref, o_ref, *scratch):
    if scratch:
        ab_ref, = scratch

        @pl.when(pl.program_id(1) == 0)
        def _():
            ab_ref[...] = a_ref[...].astype(BF16)
    else:
        ab_ref = a_ref

    o_ref[...] = r_ref[...] + _dot(ab_ref[...], w_ref[...])


def _mm_res(a, w, layer, resid):
    m, k = a.shape
    n = w.shape[-1]
    tm = _row_tile(m, 1024)
    tn = 1024
    scratch = [] if a.dtype == BF16 else [pltpu.VMEM((tm, k), BF16)]
    return pl.pallas_call(
        _mm_res_body,
        grid=(m // tm, n // tn),
        in_specs=[
            pl.BlockSpec((tm, k), lambda i, j: (i, 0)),
            pl.BlockSpec((None, k, tn), lambda i, j: (layer, 0, j)),
            pl.BlockSpec((tm, tn), lambda i, j: (i, j)),
        ],
        out_specs=pl.BlockSpec((tm, tn), lambda i, j: (i, j)),
        out_shape=jax.ShapeDtypeStruct((m, n), F32),
        scratch_shapes=scratch,
        compiler_params=_params("arbitrary", "arbitrary"),
        name="mm_res",
    )(a, w, resid)


def _proj_conv_body(x_ref, g_ref, w_ref, b_ref, cw_ref, cb_ref, buf_ref, o_ref, nb_ref,
                    carry_ref, pad_ref, *, chunk, tiles_per_seq):
    i = pl.program_id(0)
    tm = o_ref.shape[0]
    lo = SUBLANES - (CONV_W - 1)

    @pl.when(i == 0)
    def _():
        carry_ref[...] = jnp.zeros_like(carry_ref)

    first = (i % tiles_per_seq) == 0
    xn = _rms(x_ref[...], g_ref[...]).astype(BF16)
    for c0 in range(0, o_ref.shape[-1], chunk):
        cols = slice(c0, c0 + chunk)
        u = _dot(xn, w_ref[:, cols]) + b_ref[:, cols]
        pad_ref[lo:SUBLANES, :] = jnp.where(first, buf_ref[:, cols], carry_ref[0:CONV_W - 1, cols])
        pad_ref[SUBLANES:SUBLANES + tm, :] = u
        out = cb_ref[:, cols] + pad_ref[lo:lo + tm, :] * cw_ref[0:1, cols]
        for j in range(1, CONV_W):
            out = out + pad_ref[lo + j:lo + j + tm, :] * cw_ref[j:j + 1, cols]
        o_ref[:, cols] = out
        last = pad_ref[tm + lo:tm + SUBLANES, :]
        carry_ref[0:CONV_W - 1, cols] = last
        nb_ref[:, cols] = last


def _proj_body(x_ref, g_ref, w_ref, *rest, mode, chunk, conv_args=None):
    if mode == "conv":
        _proj_conv_body(x_ref, g_ref, w_ref, *rest, chunk=chunk, **conv_args)
        return
    o_ref = rest[-1]
    xn = _rms(x_ref[...], g_ref[...]).astype(BF16)
    for c0 in range(0, o_ref.shape[-1], chunk):
        y = _dot(xn, w_ref[:, c0:c0 + chunk])
        if mode in ("q", "k"):
            c, sa, sb = rest[0][...], rest[1][...], rest[2][...]
            parts = []
            for l0 in range(0, chunk, LANES):
                yc = y[:, l0:l0 + LANES]
                parts.append(yc * c + pltpu.roll(yc, LANES - ROT_DIM // 2, 1) * sa
                             + pltpu.roll(yc, ROT_DIM // 2, 1) * sb)
            y = jnp.concatenate(parts, axis=1)
            if mode == "q":
                y = y * (HEAD_DIM ** -0.5)
        elif mode in ("bias", "gelu"):
            y = y + rest[0][:, c0:c0 + chunk]
            if mode == "gelu":
                y = jax.nn.gelu(y)
        o_ref[:, c0:c0 + chunk] = y.astype(o_ref.dtype)


def _proj_row_tile(m, n):
    return _row_tile(m, 1024 if n <= D_MODEL else 512)


def _proj(x, g, w, layer, mode, *, tabs=None, period_rows=None, bias=None, conv=None, stack=None):
    m = x.shape[0]
    n = w.shape[-1]
    tm = _proj_row_tile(m, n)
    chunk = 512 if n % 512 == 0 else n // 3
    extra, extra_specs, scratch, conv_args = [], [], [], None
    out_specs = pl.BlockSpec((tm, n), lambda i: (i, 0))
    out_shape = jax.ShapeDtypeStruct((m, n), BF16 if mode == "q" else F32)
    vec_spec = pl.BlockSpec((None, 1, n), lambda i: (layer, 0, 0))
    if mode in ("q", "k"):
        pb = period_rows // tm
        extra = list(tabs)
        extra_specs = [pl.BlockSpec((tm, LANES), lambda i: (i % pb, 0))] * 3
    elif mode in ("bias", "gelu"):
        extra = [bias]
        extra_specs = [vec_spec]
    elif mode == "conv":
        cw, cb, buf, t_len = conv
        tiles_per_seq = t_len // tm
        assert tiles_per_seq * tm == t_len
        conv_args = dict(tiles_per_seq=tiles_per_seq)
        seq_spec = pl.BlockSpec((None, CONV_W - 1, n), lambda i: (i // tiles_per_seq, 0, 0))
        extra = [bias, cw, cb, buf]
        extra_specs = [vec_spec, pl.BlockSpec((None, CONV_W, n), lambda i: (layer, 0, 0)),
                       vec_spec, seq_spec]
        out_specs = [out_specs, seq_spec]
        out_shape = [out_shape, jax.ShapeDtypeStruct(buf.shape, F32)]
        scratch = [pltpu.VMEM((SUBLANES, n), F32), pltpu.VMEM((tm + SUBLANES, chunk), F32)]
    aliases = {}
    if stack is not None:
        n_slots, slot, prev = stack
        out_specs = pl.BlockSpec((None, tm, n), lambda i: (slot, i, 0))
        out_shape = jax.ShapeDtypeStruct((n_slots, m, n), out_shape.dtype)
        if prev is not None:
            aliases = {3 + len(extra): 0}
            extra = extra + [prev]
            extra_specs = extra_specs + [pl.BlockSpec(memory_space=pl.ANY)]
    body = functools.partial(_proj_body, mode=mode, chunk=chunk, conv_args=conv_args)
    return pl.pallas_call(
        body,
        grid=(m // tm,),
        in_specs=[
            pl.BlockSpec((tm, D_MODEL), lambda i: (i, 0)),
            pl.BlockSpec((1, D_MODEL), lambda i: (0, 0)),
            pl.BlockSpec((None, D_MODEL, n), lambda i: (layer, 0, 0), pipeline_mode=pl.Buffered(1)),
            *extra_specs,
        ],
        out_specs=out_specs,
        out_shape=out_shape,
        scratch_shapes=scratch,
        input_output_aliases=aliases,
        compiler_params=_params("arbitrary"),
        name="proj_" + mode,
    )(x, g, w, *extra)


def _lam_from(lp):
    s1 = jnp.sum(lp[0:1, :] * lp[1:2, :], axis=-1, keepdims=True)
    s2 = jnp.sum(lp[2:3, :] * lp[3:4, :], axis=-1, keepdims=True)
    return jnp.exp(s1) - jnp.exp(s2)


def _pattn_body(lp_ref, g_ref, q_ref, k_ref, v_ref, o_ref, kb_ref, vb_ref, *, tq, lam_init):
    qi = pl.program_id(2)

    @pl.when(qi == 0)
    def _():
        kb_ref[...] = k_ref[...].astype(BF16)
        vb_ref[...] = v_ref[...].astype(BF16)

    q = q_ref[...]
    lane = lax.broadcasted_iota(jnp.int32, (tq, V_DIM), 1)
    zero = jnp.zeros_like(q)
    qs = jnp.concatenate([jnp.where(lane < HEAD_DIM, q, zero),
                          jnp.where(lane >= HEAD_DIM, q, zero)], axis=0)

    def scores(j):
        kt = kb_ref[pl.ds(pl.multiple_of(j * tq, tq), tq), :]
        return _dot_nt(qs, kt)

    def values(j):
        return vb_ref[pl.ds(pl.multiple_of(j * tq, tq), tq), :]

    row = lax.broadcasted_iota(jnp.int32, (2 * tq, tq), 0)
    col = lax.broadcasted_iota(jnp.int32, (2 * tq, tq), 1)
    qrow = jnp.where(row >= tq, row - tq, row)
    s = jnp.where(col <= qrow, scores(qi), NEG_INF)
    m0 = jnp.max(s, axis=-1, keepdims=True)
    p = jnp.exp(s - m0)
    l0 = jnp.sum(p, axis=-1, keepdims=True)
    acc0 = _dot(p.astype(BF16), values(qi))

    def body(j, carry):
        m, l, acc = carry
        s = scores(j)
        m_new = jnp.maximum(m, jnp.max(s, axis=-1, keepdims=True))
        alpha = jnp.exp(m - m_new)
        p = jnp.exp(s - m_new)
        l = alpha * l + jnp.sum(p, axis=-1, keepdims=True)
        acc = alpha * acc + _dot(p.astype(BF16), values(j))
        return m_new, l, acc

    _, l, acc = lax.fori_loop(0, qi, body, (m0, l0, acc0))
    o = acc / l
    lam = _lam_from(lp_ref[...]) + lam_init
    o = o[:tq] - lam * o[tq:]
    o_ref[...] = (_rms(o, g_ref[...]) * (1.0 - lam_init)).astype(BF16)


def _prompt_attention(q, k, v, slot, lp, g, batch, seq, lam_init):
    tq = 512
    nq = seq // tq
    body = functools.partial(_pattn_body, tq=tq, lam_init=lam_init)
    return pl.pallas_call(
        body,
        grid=(batch, N_HEADS, nq),
        in_specs=[
            pl.BlockSpec((4, HEAD_DIM), lambda b, h, i: (0, 0)),
            pl.BlockSpec((1, V_DIM), lambda b, h, i: (0, 0)),
            pl.BlockSpec((tq, V_DIM), lambda b, h, i: (b * nq + i, h)),
            pl.BlockSpec((None, seq, V_DIM), lambda b, h, i: (slot, b, h)),
            pl.BlockSpec((None, seq, V_DIM), lambda b, h, i: (slot, b, h)),
        ],
        out_specs=pl.BlockSpec((tq, V_DIM), lambda b, h, i: (b * nq + i, h)),
        out_shape=jax.ShapeDtypeStruct((batch * seq, ATTN_WIDTH), BF16),
        scratch_shapes=[pltpu.VMEM((seq, V_DIM), BF16), pltpu.VMEM((seq, V_DIM), BF16)],
        compiler_params=_params("arbitrary", "arbitrary", "arbitrary"),
        name="prompt_attn",
    )(lp, g, q, k, v)


HEAD_GROUP = SUBLANES
N_GROUPS = N_HEADS // HEAD_GROUP
PAGES_PER_STEP = 8


def _dattn_body(pt_ref, lp_ref, g_ref, q_ref, kn_ref, vn_ref, *rest,
                n_steps, page, dec_seq, lam_init):
    del pt_ref
    kp_refs = rest[:PAGES_PER_STEP]
    vp_refs = rest[PAGES_PER_STEP:2 * PAGES_PER_STEP]
    o_ref, m_ref, l_ref, acc_ref, bias_ref = rest[2 * PAGES_PER_STEP:]
    step = pl.program_id(1)
    rows = 2 * dec_seq * HEAD_GROUP

    def group(ref, gi):
        x = ref[:, gi * HEAD_GROUP:(gi + 1) * HEAD_GROUP, :]
        return x.reshape(x.shape[0] * HEAD_GROUP, V_DIM).astype(BF16)

    def qg(gi):
        return q_ref[gi * rows:(gi + 1) * rows, :]

    @pl.when(step == 0)
    def _():
        r = lax.broadcasted_iota(jnp.int32, (rows, page * HEAD_GROUP), 0)
        c = lax.broadcasted_iota(jnp.int32, (rows, page * HEAD_GROUP), 1)
        same_head = (r & (HEAD_GROUP - 1)) == (c & (HEAD_GROUP - 1))
        bias_ref[...] = jnp.where(same_head, 0.0, NEG_INF)
        nk = dec_seq * HEAD_GROUP
        r2 = lax.broadcasted_iota(jnp.int32, (rows, nk), 0)
        c2 = lax.broadcasted_iota(jnp.int32, (rows, nk), 1)
        ok = ((r2 & (HEAD_GROUP - 1)) == (c2 & (HEAD_GROUP - 1))) & (
            lax.shift_right_logical(c2, 3) <= lax.shift_right_logical(r2, 4))
        for gi in range(N_GROUPS):
            sl = slice(gi * rows, (gi + 1) * rows)
            s = jnp.where(ok, _dot_nt(qg(gi), group(kn_ref, gi)), NEG_INF)
            m = jnp.max(s, axis=-1, keepdims=True)
            pr = jnp.exp(s - m)
            m_ref[sl, :] = m
            l_ref[sl, :] = jnp.sum(pr, axis=-1, keepdims=True)
            acc_ref[sl, :] = _dot(pr.astype(BF16), group(vn_ref, gi))

    bias = bias_ref[...]
    for gi in range(N_GROUPS):
        sl = slice(gi * rows, (gi + 1) * rows)
        q = qg(gi)
        ss = [_dot_nt(q, group(kp, gi)) + bias for kp in kp_refs]
        m_old = m_ref[sl, :]
        m_new = m_old
        for s in ss:
            m_new = jnp.maximum(m_new, jnp.max(s, axis=-1, keepdims=True))
        alpha = jnp.exp(m_old - m_new)
        l_new = alpha * l_ref[sl, :]
        acc = alpha * acc_ref[sl, :]
        for s, vp in zip(ss, vp_refs):
            pr = jnp.exp(s - m_new)
            l_new = l_new + jnp.sum(pr, axis=-1, keepdims=True)
            acc = acc + _dot(pr.astype(BF16), group(vp, gi))
        m_ref[sl, :] = m_new
        l_ref[sl, :] = l_new
        acc_ref[sl, :] = acc

    @pl.when(step == n_steps - 1)
    def _():
        o = acc_ref[...] / l_ref[...]
        lam = _lam_from(lp_ref[...]) + lam_init
        g = g_ref[...]
        for gi in range(N_GROUPS):
            for t in range(dec_seq):
                base = gi * rows + 2 * t * HEAD_GROUP
                o1 = o[base:base + HEAD_GROUP]
                o2 = o[base + HEAD_GROUP:base + 2 * HEAD_GROUP]
                o_ref[t, gi * HEAD_GROUP:(gi + 1) * HEAD_GROUP, :] = (
                    _rms(o1 - lam * o2, g) * (1.0 - lam_init))


def _sample_attention(qall, k_new, v_new, cache_k, cache_v, slot, page_table, lp, g, lam_init):
    db, n_pages = page_table.shape
    page = cache_k.shape[2]
    dec_seq = k_new.shape[1]
    nq = 2 * dec_seq * N_HEADS
    n_steps = n_pages // PAGES_PER_STEP
    body = functools.partial(_dattn_body, n_steps=n_steps, page=page, dec_seq=dec_seq,
                             lam_init=lam_init)

    def page_spec(i):
        return pl.BlockSpec(
            (None, None, page, N_HEADS, V_DIM),
            lambda b, s, pt: (slot, pt[b * n_pages + s * PAGES_PER_STEP + i], 0, 0, 0))

    page_specs = [page_spec(i) for i in range(PAGES_PER_STEP)]
    new_spec = pl.BlockSpec((None, dec_seq, N_HEADS, V_DIM), lambda b, s, pt: (b, 0, 0, 0))
    grid_spec = pltpu.PrefetchScalarGridSpec(
        num_scalar_prefetch=1,
        grid=(db, n_steps),
        in_specs=[
            pl.BlockSpec((4, HEAD_DIM), lambda b, s, pt: (0, 0)),
            pl.BlockSpec((1, V_DIM), lambda b, s, pt: (0, 0)),
            pl.BlockSpec((None, nq, V_DIM), lambda b, s, pt: (b, 0, 0)),
            new_spec, new_spec, *page_specs, *page_specs,
        ],
        out_specs=pl.BlockSpec((None, dec_seq, N_HEADS, V_DIM), lambda b, s, pt: (b, 0, 0, 0)),
        scratch_shapes=[
            pltpu.VMEM((nq, 1), F32), pltpu.VMEM((nq, 1), F32), pltpu.VMEM((nq, V_DIM), F32),
            pltpu.VMEM((nq // N_GROUPS, page * HEAD_GROUP), F32),
        ],
    )
    return pl.pallas_call(
        body,
        grid_spec=grid_spec,
        out_shape=jax.ShapeDtypeStruct((db, dec_seq, N_HEADS, V_DIM), F32),
        compiler_params=_params("arbitrary", "arbitrary"),
        name="sample_attn",
    )(page_table.reshape(-1), lp, g, qall, k_new, v_new,
      *([cache_k] * PAGES_PER_STEP), *([cache_v] * PAGES_PER_STEP))


def _conv_body(u_ref, buf_ref, w_ref, b_ref, uc_ref, nb_ref, pad_ref, *, t_len):
    lo = SUBLANES - (CONV_W - 1)
    pad_ref[lo:SUBLANES, :] = buf_ref[...]
    pad_ref[SUBLANES:SUBLANES + t_len, :] = u_ref[...]
    out = b_ref[...] + pad_ref[lo:lo + t_len, :] * w_ref[0:1, :]
    for j in range(1, CONV_W):
        out = out + pad_ref[lo + j:lo + j + t_len, :] * w_ref[j:j + 1, :]
    uc_ref[...] = out
    nb_ref[...] = pad_ref[t_len + lo:t_len + SUBLANES, :]


def _conv(u, buf, w, b, layer):
    bsz, t_len, _ = u.shape
    tc = 384
    body = functools.partial(_conv_body, t_len=t_len)
    return pl.pallas_call(
        body,
        grid=(bsz, D_RNN // tc),
        in_specs=[
            pl.BlockSpec((None, t_len, tc), lambda b_, c: (b_, 0, c)),
            pl.BlockSpec((None, CONV_W - 1, tc), lambda b_, c: (b_, 0, c)),
            pl.BlockSpec((None, CONV_W, tc), lambda b_, c: (layer, 0, c)),
            pl.BlockSpec((None, 1, tc), lambda b_, c: (layer, 0, c)),
        ],
        out_specs=[
            pl.BlockSpec((None, t_len, tc), lambda b_, c: (b_, 0, c)),
            pl.BlockSpec((None, CONV_W - 1, tc), lambda b_, c: (b_, 0, c)),
        ],
        out_shape=[jax.ShapeDtypeStruct(u.shape, F32), jax.ShapeDtypeStruct(buf.shape, F32)],
        scratch_shapes=[pltpu.VMEM((t_len + SUBLANES, tc), F32)],
        compiler_params=_params("arbitrary", "arbitrary"),
        name="conv",
    )(u, buf, w, b)


def _softplus(z):
    return jnp.maximum(z, 0.0) + jnp.log1p(jnp.exp(-jnp.abs(z)))


GATE_TN = 3 * LANES
GATE_TK = 7 * LANES


def _gate_window(j):
    first_block = (GATE_TN * j) // RG_BLOCK
    k0 = (RG_BLOCK * first_block) // LANES * LANES
    smallest = min if isinstance(j, int) else jnp.minimum
    return smallest(k0, D_RNN - GATE_TK)


def _banded(w):
    n_l, n_h, bi, bj = w.shape
    w = w.astype(BF16)
    dense = jnp.concatenate(
        [jnp.pad(w[:, h], ((0, 0), (0, 0), (h * bj, (n_h - 1 - h) * bj))) for h in range(n_h)], axis=1)
    tiles = []
    for j in range(D_RNN // GATE_TN):
        k0 = _gate_window(j)
        lo_block = (GATE_TN * j) // RG_BLOCK
        hi_block = (GATE_TN * (j + 1) - 1) // RG_BLOCK
        assert k0 <= RG_BLOCK * lo_block and RG_BLOCK * (hi_block + 1) <= k0 + GATE_TK
        tiles.append(dense[:, k0:k0 + GATE_TK, GATE_TN * j:GATE_TN * (j + 1)])
    return jnp.stack(tiles, axis=1)


def _gates_body(uc_ref, wa_ref, wx_ref, ba_ref, bx_ref, lam_ref, a_ref, xin_ref, ub_ref):
    j = pl.program_id(1)

    @pl.when(j == 0)
    def _():
        ub_ref[...] = uc_ref[...].astype(BF16)

    ub = ub_ref[:, pl.ds(pl.multiple_of(_gate_window(j), LANES), GATE_TK)]
    r = jax.nn.sigmoid(_dot(ub, wa_ref[...]) + ba_ref[...])
    i = jax.nn.sigmoid(_dot(ub, wx_ref[...]) + bx_ref[...])
    log_a = -RG_C * r * _softplus(-lam_ref[...])
    a = jnp.exp(log_a)
    u = uc_ref[:, pl.ds(pl.multiple_of(j * GATE_TN, LANES), GATE_TN)]
    a_ref[...] = a
    xin_ref[...] = jnp.sqrt(1.0 - a * a) * i * u


def _gates(uc, wa_b, wx_b, ba, bx, lam, layer):
    m = uc.shape[0]
    tm = _row_tile(m, 1024)
    vec = pl.BlockSpec((None, 1, GATE_TN), lambda i, j: (layer, 0, j))
    wsp = pl.BlockSpec((None, None, GATE_TK, GATE_TN), lambda i, j: (layer, j, 0, 0))
    osp = pl.BlockSpec((tm, GATE_TN), lambda i, j: (i, j))
    return pl.pallas_call(
        _gates_body,
        grid=(m // tm, D_RNN // GATE_TN),
        in_specs=[pl.BlockSpec((tm, D_RNN), lambda i, j: (i, 0)), wsp, wsp, vec, vec, vec],
        out_specs=[osp, osp],
        out_shape=[jax.ShapeDtypeStruct((m, D_RNN), F32), jax.ShapeDtypeStruct((m, D_RNN), F32)],
        scratch_shapes=[pltpu.VMEM((tm, D_RNN), BF16)],
        compiler_params=_params("arbitrary", "arbitrary"),
        name="gates",
    )(uc, wa_b, wx_b, ba, bx, lam)


def _gates_scan_body(uc_ref, wa_ref, wx_ref, ba_ref, bx_ref, lam_ref, gate_ref, h0_ref,
                     y_ref, ht_ref, ub_ref, a_ref, xin_ref, carry_ref, *, tiles_per_seq):
    i, j = pl.program_id(0), pl.program_id(1)

    @pl.when((i == 0) & (j == 0))
    def _():
        carry_ref[...] = jnp.zeros_like(carry_ref)

    _gates_body(uc_ref, wa_ref, wx_ref, ba_ref, bx_ref, lam_ref, a_ref, xin_ref, ub_ref)
    cols = pl.ds(pl.multiple_of(j * GATE_TN, LANES), GATE_TN)
    h = jnp.where(i % tiles_per_seq == 0, h0_ref[...], carry_ref[0:1, cols])
    h = _scan_rows(a_ref, xin_ref, gate_ref, y_ref, h, a_ref.shape[0])
    carry_ref[0:1, cols] = h
    ht_ref[...] = h


def _gates_scan(uc, gate, h0, wa_b, wx_b, ba, bx, lam, layer, t_len):
    m = uc.shape[0]
    tm = _row_tile(m, 1024)
    tiles_per_seq = t_len // tm
    assert tiles_per_seq * tm == t_len
    body = functools.partial(_gates_scan_body, tiles_per_seq=tiles_per_seq)
    vec = pl.BlockSpec((None, 1, GATE_TN), lambda i, j: (layer, 0, j))
    wsp = pl.BlockSpec((None, None, GATE_TK, GATE_TN), lambda i, j: (layer, j, 0, 0))
    tile = pl.BlockSpec((tm, GATE_TN), lambda i, j: (i, j))
    hsp = pl.BlockSpec((None, 1, GATE_TN), lambda i, j: (i // tiles_per_seq, 0, j))
    y, h_tiles = pl.pallas_call(
        body,
        grid=(m // tm, D_RNN // GATE_TN),
        in_specs=[pl.BlockSpec((tm, D_RNN), lambda i, j: (i, 0)), wsp, wsp, vec, vec, vec, tile, hsp],
        out_specs=[tile, pl.BlockSpec((None, 1, GATE_TN), lambda i, j: (i, 0, j))],
        out_shape=[jax.ShapeDtypeStruct((m, D_RNN), BF16),
                   jax.ShapeDtypeStruct((m // tm, 1, D_RNN), F32)],
        scratch_shapes=[pltpu.VMEM((tm, D_RNN), BF16), pltpu.VMEM((tm, GATE_TN), F32),
                        pltpu.VMEM((tm, GATE_TN), F32), pltpu.VMEM((SUBLANES, D_RNN), F32)],
        compiler_params=_params("arbitrary", "arbitrary"),
        name="gates_scan",
    )(uc, wa_b, wx_b, ba, bx, lam, gate, h0)
    return y, h_tiles[tiles_per_seq - 1::tiles_per_seq]


def _scan_rows(a_ref, x_ref, gate_ref, y_ref, h, t_len):
    tc = a_ref.shape[-1]
    step = 2 * SUBLANES
    if t_len % step == 0:
        row = lax.broadcasted_iota(jnp.int32, (SUBLANES, tc), 0)

        def chunk(sl, h):
            a = a_ref[sl, :]
            x = x_ref[sl, :]
            d = 1
            while d < SUBLANES:
                valid = row >= d
                x = jnp.where(valid, a * pltpu.roll(x, d, 0) + x, x)
                a = jnp.where(valid, a * pltpu.roll(a, d, 0), a)
                d *= 2
            return a * h + x

        def body(c, h):
            base = pl.multiple_of(c * step, step)
            hs1 = chunk(pl.ds(base, SUBLANES), h)
            hs2 = chunk(pl.ds(pl.multiple_of(base + SUBLANES, SUBLANES), SUBLANES),
                        hs1[SUBLANES - 1:SUBLANES, :])
            sl = pl.ds(base, step)
            y_ref[sl, :] = (jnp.concatenate([hs1, hs2], axis=0) * gate_ref[sl, :]).astype(y_ref.dtype)
            return hs2[SUBLANES - 1:SUBLANES, :]

        h = lax.fori_loop(0, t_len // step, body, h)
    else:
        for t in range(t_len):
            h = a_ref[t:t + 1, :] * h + x_ref[t:t + 1, :]
            y_ref[t:t + 1, :] = (h * gate_ref[t:t + 1, :]).astype(y_ref.dtype)
    return h


def _scan_body(a_ref, x_ref, gate_ref, h0_ref, y_ref, ht_ref, *, t_len):
    ht_ref[...] = _scan_rows(a_ref, x_ref, gate_ref, y_ref, h0_ref[...], t_len)


def _scan(a, xin, gate, h0):
    bsz, t_len, _ = a.shape
    tc = 384
    body = functools.partial(_scan_body, t_len=t_len)
    seq_spec = pl.BlockSpec((None, t_len, tc), lambda b_, c: (b_, 0, c))
    h_spec = pl.BlockSpec((None, 1, tc), lambda b_, c: (b_, 0, c))
    y_dtype = BF16 if t_len % (2 * SUBLANES) == 0 else F32
    return pl.pallas_call(
        body,
        grid=(bsz, D_RNN // tc),
        in_specs=[seq_spec, seq_spec, seq_spec, h_spec],
        out_specs=[seq_spec, h_spec],
        out_shape=[jax.ShapeDtypeStruct(a.shape, y_dtype), jax.ShapeDtypeStruct((bsz, 1, D_RNN), F32)],
        compiler_params=_params("arbitrary", "arbitrary"),
        name="scan",
    )(a, xin, gate, h0)


def _rope_tables(pos):
    half = ROT_DIM // 2
    inv = ROPE_THETA ** (-jnp.arange(0, ROT_DIM, 2, dtype=F32) / ROT_DIM)
    ang = pos.astype(F32)[:, None] * inv[None, :]
    cos, sin = jnp.cos(ang), jnp.sin(ang)
    t_len = pos.shape[0]
    ones = jnp.ones((t_len, HEAD_DIM - ROT_DIM), F32)
    zer = jnp.zeros((t_len, HEAD_DIM - ROT_DIM), F32)
    zh = jnp.zeros((t_len, half), F32)
    c = jnp.concatenate([cos, cos, ones], axis=1)
    sa = jnp.concatenate([-sin, zh, zer], axis=1)
    sb = jnp.concatenate([zh, sin, zer], axis=1)
    return tuple(jnp.concatenate([t, t], axis=1) for t in (c, sa, sb))


def _trunk(x, bsz, t_len, p, ffn_w, tabs, period_rows, attn_fn, h0, buf0):
    hs, bufs, emitted = [], [], []
    n_attn = (DEPTH + 1) // 2
    ks = jnp.zeros((n_attn, x.shape[0], ATTN_WIDTH), F32)
    vs = jnp.zeros((n_attn, x.shape[0], ATTN_WIDTH), F32)

    def ffn(x, g, which, l, gf=None):
        if ffn_w is not None:
            return _ffn(x, g, *ffn_w[l][which], gf=gf), None
        names = ("pre_w1", "pre_w3", "pre_w2") if which == 0 else ("post_w1", "post_w3", "post_w2")
        y, *wb = _ffn(x, g, *(p[n] for n in names), gf=gf, layer=l)
        return y, tuple(wb)

    for l in range(DEPTH):
        j = l // 2
        x, wb_pre = ffn(x, p["ln_ffn_pre"][l:l + 1], 0, l)
        g_mix = p["ln_mix"][l:l + 1]
        if l % 2 == 0:
            lam_init = 0.8 - 0.6 * math.exp(-0.3 * l)
            q = _proj(x, g_mix, p["wq"], j, "q", tabs=tabs, period_rows=period_rows)
            ks = _proj(x, g_mix, p["wk"], j, "k", tabs=tabs, period_rows=period_rows,
                       stack=(n_attn, j, ks))
            vs = _proj(x, g_mix, p["wv"], j, "v", stack=(n_attn, j, vs))
            o = attn_fn(j, q, ks, vs, p["lp"][j], p["attn_subln"][j:j + 1], lam_init)
            x = _mm_res(o, p["wo"], j, x)
        else:
            gate = _proj(x, g_mix, p["w_gate"], j, "gelu", bias=p["b_gate"])
            if t_len % _proj_row_tile(bsz * t_len, D_RNN) == 0:
                uc, nb = _proj(x, g_mix, p["w_in"], j, "conv", bias=p["b_in"],
                               conv=(p["conv_w"], p["conv_b"], buf0[j], t_len))
            else:
                u = _proj(x, g_mix, p["w_in"], j, "bias", bias=p["b_in"])
                uc, nb = _conv(u.reshape(bsz, t_len, D_RNN), buf0[j], p["conv_w"], p["conv_b"], j)
            gate_w = (p["wa_b"], p["wx_b"], p["ba"], p["bx"], p["lam"], j)
            uc = uc.reshape(bsz * t_len, D_RNN)
            if t_len % _row_tile(bsz * t_len, 1024) == 0:
                yg, ht = _gates_scan(uc, gate, h0[j][:, None, :], *gate_w, t_len)
            else:
                a, xin = _gates(uc, *gate_w)
                yg, ht = _scan(a.reshape(bsz, t_len, D_RNN), xin.reshape(bsz, t_len, D_RNN),
                               gate.reshape(bsz, t_len, D_RNN), h0[j][:, None, :])
            x = _mm_res(yg.reshape(bsz * t_len, D_RNN), p["w_out"], j, x)
            hs.append(ht[:, 0, :])
            bufs.append(nb)
        gf = p["ln_final"] if l == DEPTH - 1 else None
        x, wb_post = ffn(x, p["ln_ffn_post"][l:l + 1], 1, l, gf)
        emitted.append((wb_pre, wb_post))
    return x, ks, vs, hs, bufs, emitted


def kernel(x_prompt, x_sample, cache_k, cache_v, state_h, state_conv, page_table, ln_ffn_pre, ffn_pre_w1, ffn_pre_w3, ffn_pre_w2, ln_mix, ln_ffn_post, ffn_post_w1, ffn_post_w3, ffn_post_w2, attn_wq, attn_wk, attn_wv, attn_lq1, attn_lk1, attn_lq2, attn_lk2, attn_subln, attn_wo, rec_w_gate, rec_b_gate, rec_w_in, rec_b_in, rec_conv_w, rec_conv_b, rec_wa, rec_ba, rec_wx, rec_bx, rec_lam, rec_w_out, ln_final):
    n_b = rec_lam.shape[0]
    bsz, seq = x_prompt.shape[0], x_prompt.shape[1]
    db, dec_seq = x_sample.shape[0], x_sample.shape[1]
    past = page_table.shape[1] * cache_k.shape[2]

    p = {
        "ln_ffn_pre": ln_ffn_pre, "ln_mix": ln_mix, "ln_ffn_post": ln_ffn_post,
        "ln_final": ln_final[None, :],
        "pre_w1": ffn_pre_w1, "pre_w3": ffn_pre_w3, "pre_w2": ffn_pre_w2,
        "post_w1": ffn_post_w1, "post_w3": ffn_post_w3, "post_w2": ffn_post_w2,
        "wq": attn_wq.astype(BF16), "wk": attn_wk.astype(BF16), "wv": attn_wv.astype(BF16),
        "lp": jnp.stack([attn_lq1, attn_lk1, attn_lq2, attn_lk2], axis=1).astype(F32),
        "attn_subln": attn_subln, "wo": attn_wo.astype(BF16),
        "w_gate": rec_w_gate.astype(BF16), "w_in": rec_w_in.astype(BF16),
        "b_gate": rec_b_gate[:, None, :], "b_in": rec_b_in[:, None, :],
        "conv_w": rec_conv_w, "conv_b": rec_conv_b[:, None, :],
        "wa_b": _banded(rec_wa), "wx_b": _banded(rec_wx),
        "ba": rec_ba[:, None, :], "bx": rec_bx[:, None, :], "lam": rec_lam[:, None, :],
        "w_out": rec_w_out.astype(BF16),
    }

    def sample_mix(j, q, k, v, lp, g, lam_init):
        q5 = q.reshape(db, dec_seq, N_HEADS, 2, HEAD_DIM)
        zero = jnp.zeros_like(q5[:, :, :, 0])
        q1 = jnp.concatenate([q5[:, :, :, 0], zero], axis=-1)
        q2 = jnp.concatenate([zero, q5[:, :, :, 1]], axis=-1)
        qall = jnp.stack([q1, q2], axis=2).reshape(db, dec_seq, 2, N_GROUPS, HEAD_GROUP, V_DIM)
        qall = qall.transpose(0, 3, 1, 2, 4, 5).reshape(db, dec_seq * 2 * N_HEADS, V_DIM)
        o = _sample_attention(qall, k[j].reshape(db, dec_seq, N_HEADS, V_DIM),
                              v[j].reshape(db, dec_seq, N_HEADS, V_DIM), cache_k, cache_v, j,
                              page_table, lp, g, lam_init)
        return o.reshape(db * dec_seq, ATTN_WIDTH)

    pos_s = past + jnp.tile(jnp.arange(dec_seq), db)
    y_s, k_s, v_s, h_s, c_s, ffn_w = _trunk(
        x_sample.reshape(db * dec_seq, D_MODEL), db, dec_seq, p, None,
        _rope_tables(pos_s), db * dec_seq, sample_mix, state_h, state_conv)

    def prompt_mix(j, q, k, v, lp, g, lam_init):
        return _prompt_attention(q, k, v, j, lp, g, bsz, seq, lam_init)

    h0_p = jnp.zeros((n_b, bsz, D_RNN), F32)
    buf0_p = jnp.zeros((n_b, bsz, CONV_W - 1, D_RNN), F32)
    y_p, k_p, v_p, h_p, c_p, _ = _trunk(
        x_prompt.reshape(bsz * seq, D_MODEL), bsz, seq, p, ffn_w,
        _rope_tables(jnp.arange(seq)), seq, prompt_mix, h0_p, buf0_p)

    def kv(xs, b_, t_):
        return xs.reshape(xs.shape[0], b_, t_, N_HEADS, V_DIM)

    return (y_p.reshape(bsz, seq, D_MODEL), y_s.reshape(db, dec_seq, D_MODEL),
            kv(k_p, bsz, seq), kv(v_p, bsz, seq), jnp.stack(h_p), jnp.stack(c_p),
            kv(k_s, db, dec_seq), kv(v_s, db, dec_seq), jnp.stack(h_s), jnp.stack(c_s))
```

```python
import functools
import math

import jax
import jax.numpy as jnp
from jax import lax
from jax.experimental import pallas as pl
from jax.experimental.pallas import tpu as pltpu

F32 = jnp.float32
BF16 = jnp.bfloat16

D_MODEL = 2048
DEPTH = 4
N_HEADS = 16
HEAD_DIM = 64
V_DIM = 2 * HEAD_DIM
ATTN_WIDTH = N_HEADS * V_DIM
ROT_DIM = HEAD_DIM // 4
ROPE_THETA = 500000.0
D_RNN = 2688
RG_HEADS = 16
RG_BLOCK = D_RNN // RG_HEADS
CONV_W = 4
RG_C = 8.0
D_FF = 5632
EPS = 1e-6

LANES = 128
SUBLANES = 8
VMEM_LIMIT_BYTES = 56 * 1024 * 1024
NEG_INF = float("-inf")


def _params(*sem):
    return pltpu.CompilerParams(dimension_semantics=sem, vmem_limit_bytes=VMEM_LIMIT_BYTES)


def _rms(x, g):
    ms = jnp.mean(x * x, axis=-1, keepdims=True)
    return x * lax.rsqrt(ms + EPS) * g


def _dot(a, b):
    return jnp.dot(a, b, preferred_element_type=F32)


def _dot_nt(a, b):
    return lax.dot_general(a, b, (((1,), (1,)), ((), ())), preferred_element_type=F32)


def _row_tile(m, cap):
    return cap if m % cap == 0 else m


def _ffn_body(x_ref, g_ref, w1_ref, w3_ref, w2_ref, gf_ref, o_ref, *rest, n_f, final_norm, emit):
    j = pl.program_id(1)
    xn_ref = rest[-1]

    @pl.when(j == 0)
    def _():
        xn_ref[...] = _rms(x_ref[...], g_ref[...]).astype(BF16)
        o_ref[...] = jnp.zeros_like(o_ref)

    w1, w3, w2 = w1_ref[...], w3_ref[...], w2_ref[...]
    if emit:
        w1, w3, w2 = w1.astype(BF16), w3.astype(BF16), w2.astype(BF16)
        rest[0][...], rest[1][...], rest[2][...] = w1, w3, w2
    xn = xn_ref[...]
    a = _dot(xn, w1)
    b = _dot(xn, w3)
    h = (jax.nn.silu(a) * b).astype(BF16)
    o_ref[...] += _dot(h, w2)

    @pl.when(j == n_f - 1)
    def _():
        y = x_ref[...] + 0.5 * o_ref[...]
        if final_norm:
            y = _rms(y, gf_ref[...])
        o_ref[...] = y


def _ffn(x, g, w1, w3, w2, gf=None, layer=None):
    m = x.shape[0]
    tm = _row_tile(m, 1024)
    tf = 512
    n_f = D_FF // tf
    emit = layer is not None
    assert not emit or m == tm
    final_norm = gf is not None
    if gf is None:
        gf = g
    body = functools.partial(_ffn_body, n_f=n_f, final_norm=final_norm, emit=emit)
    up_spec = pl.BlockSpec((D_MODEL, tf), lambda i, j: (0, j))
    down_spec = pl.BlockSpec((tf, D_MODEL), lambda i, j: (j, 0))
    out_specs = [pl.BlockSpec((tm, D_MODEL), lambda i, j: (i, 0))]
    out_shape = [jax.ShapeDtypeStruct((m, D_MODEL), F32)]
    w_specs = [up_spec, up_spec, down_spec]
    if emit:
        out_specs += w_specs
        out_shape += [jax.ShapeDtypeStruct(w.shape[1:], BF16) for w in (w1, w3, w2)]
        w_specs = [pl.BlockSpec((None, D_MODEL, tf), lambda i, j: (layer, 0, j)),
                   pl.BlockSpec((None, D_MODEL, tf), lambda i, j: (layer, 0, j)),
                   pl.BlockSpec((None, tf, D_MODEL), lambda i, j: (layer, j, 0))]
    outs = pl.pallas_call(
        body,
        grid=(m // tm, n_f),
        in_specs=[
            pl.BlockSpec((tm, D_MODEL), lambda i, j: (i, 0), pipeline_mode=pl.Buffered(1)),
            pl.BlockSpec((1, D_MODEL), lambda i, j: (0, 0)),
            *w_specs,
            pl.BlockSpec((1, D_MODEL), lambda i, j: (0, 0)),
        ],
        out_specs=out_specs,
        out_shape=out_shape,
        scratch_shapes=[pltpu.VMEM((tm, D_MODEL), BF16)],
        compiler_params=_params("arbitrary", "arbitrary"),
        name="ffn_cast" if emit else "ffn",
    )(x, g, w1, w3, w2, gf)
    return outs if emit else outs[0]


def _mm_res_body(a_ref, w_ref, r_ref, o_ref, *scratch):
    if scratch:
        ab_ref, = scratch

        @pl.when(pl.program_id(1) == 0)
        def _():
            ab_ref[...] = a_ref[...].astype(BF16)
    else:
        ab_ref = a_ref

    o_ref[...] = r_ref[...] + _dot(ab_ref[...], w_ref[...])


def _mm_res(a, w, layer, resid):
    m, k = a.shape
    n = w.shape[-1]
    tm = _row_tile(m, 1024)
    tn = 1024
    scratch = [] if a.dtype == BF16 else [pltpu.VMEM((tm, k), BF16)]
    return pl.pallas_call(
        _mm_res_body,
        grid=(m // tm, n // tn),
        in_specs=[
            pl.BlockSpec((tm, k), lambda i, j: (i, 0)),
            pl.BlockSpec((None, k, tn), lambda i, j: (layer, 0, j)),
            pl.BlockSpec((tm, tn), lambda i, j: (i, j)),
        ],
        out_specs=pl.BlockSpec((tm, tn), lambda i, j: (i, j)),
        out_shape=jax.ShapeDtypeStruct((m, n), F32),
        scratch_shapes=scratch,
        compiler_params=_params("arbitrary", "arbitrary"),
        name="mm_res",
    )(a, w, resid)


def _proj_conv_body(x_ref, g_ref, w_ref, b_ref, cw_ref, cb_ref, buf_ref, o_ref, nb_ref,
                    carry_ref, pad_ref, *, chunk, tiles_per_seq):
    i = pl.program_id(0)
    tm = o_ref.shape[0]
    lo = SUBLANES - (CONV_W - 1)

    @pl.when(i == 0)
    def _():
        carry_ref[...] = jnp.zeros_like(carry_ref)

    first = (i % tiles_per_seq) == 0
    xn = _rms(x_ref[...], g_ref[...]).astype(BF16)
    for c0 in range(0, o_ref.shape[-1], chunk):
        cols = slice(c0, c0 + chunk)
        u = _dot(xn, w_ref[:, cols]) + b_ref[:, cols]
        pad_ref[lo:SUBLANES, :] = jnp.where(first, buf_ref[:, cols], carry_ref[0:CONV_W - 1, cols])
        pad_ref[SUBLANES:SUBLANES + tm, :] = u
        out = cb_ref[:, cols] + pad_ref[lo:lo + tm, :] * cw_ref[0:1, cols]
        for j in range(1, CONV_W):
            out = out + pad_ref[lo + j:lo + j + tm, :] * cw_ref[j:j + 1, cols]
        o_ref[:, cols] = out
        last = pad_ref[tm + lo:tm + SUBLANES, :]
        carry_ref[0:CONV_W - 1, cols] = last
        nb_ref[:, cols] = last


def _proj_body(x_ref, g_ref, w_ref, *rest, mode, chunk, conv_args=None):
    if mode == "conv":
        _proj_conv_body(x_ref, g_ref, w_ref, *rest, chunk=chunk, **conv_args)
        return
    o_ref = rest[-1]
    xn = _rms(x_ref[...], g_ref[...]).astype(BF16)
    for c0 in range(0, o_ref.shape[-1], chunk):
        y = _dot(xn, w_ref[:, c0:c0 + chunk])
        if mode in ("q", "k"):
            c, sa, sb = rest[0][...], rest[1][...], rest[2][...]
            parts = []
            for l0 in range(0, chunk, LANES):
                yc = y[:, l0:l0 + LANES]
                parts.append(yc * c + pltpu.roll(yc, LANES - ROT_DIM // 2, 1) * sa
                             + pltpu.roll(yc, ROT_DIM // 2, 1) * sb)
            y = jnp.concatenate(parts, axis=1)
            if mode == "q":
                y = y * (HEAD_DIM ** -0.5)
        elif mode in ("bias", "gelu"):
            y = y + rest[0][:, c0:c0 + chunk]
            if mode == "gelu":
                y = jax.nn.gelu(y)
        o_ref[:, c0:c0 + chunk] = y.astype(o_ref.dtype)


def _proj_row_tile(m, n):
    return _row_tile(m, 1024 if n <= D_MODEL else 512)


def _proj(x, g, w, layer, mode, *, tabs=None, period_rows=None, bias=None, conv=None, stack=None):
    m = x.shape[0]
    n = w.shape[-1]
    tm = _proj_row_tile(m, n)
    chunk = 512 if n % 512 == 0 else n // 3
    extra, extra_specs, scratch, conv_args = [], [], [], None
    out_specs = pl.BlockSpec((tm, n), lambda i: (i, 0))
    out_shape = jax.ShapeDtypeStruct((m, n), BF16 if mode == "q" else F32)
    vec_spec = pl.BlockSpec((None, 1, n), lambda i: (layer, 0, 0))
    if mode in ("q", "k"):
        pb = period_rows // tm
        extra = list(tabs)
        extra_specs = [pl.BlockSpec((tm, LANES), lambda i: (i % pb, 0))] * 3
    elif mode in ("bias", "gelu"):
        extra = [bias]
        extra_specs = [vec_spec]
    elif mode == "conv":
        cw, cb, buf, t_len = conv
        tiles_per_seq = t_len // tm
        assert tiles_per_seq * tm == t_len
        conv_args = dict(tiles_per_seq=tiles_per_seq)
        seq_spec = pl.BlockSpec((None, CONV_W - 1, n), lambda i: (i // tiles_per_seq, 0, 0))
        extra = [bias, cw, cb, buf]
        extra_specs = [vec_spec, pl.BlockSpec((None, CONV_W, n), lambda i: (layer, 0, 0)),
                       vec_spec, seq_spec]
        out_specs = [out_specs, seq_spec]
        out_shape = [out_shape, jax.ShapeDtypeStruct(buf.shape, F32)]
        scratch = [pltpu.VMEM((SUBLANES, n), F32), pltpu.VMEM((tm + SUBLANES, chunk), F32)]
    aliases = {}
    if stack is not None:
        n_slots, slot, prev = stack
        out_specs = pl.BlockSpec((None, tm, n), lambda i: (slot, i, 0))
        out_shape = jax.ShapeDtypeStruct((n_slots, m, n), out_shape.dtype)
        if prev is not None:
            aliases = {3 + len(extra): 0}
            extra = extra + [prev]
            extra_specs = extra_specs + [pl.BlockSpec(memory_space=pl.ANY)]
    body = functools.partial(_proj_body, mode=mode, chunk=chunk, conv_args=conv_args)
    return pl.pallas_call(
        body,
        grid=(m // tm,),
        in_specs=[
            pl.BlockSpec((tm, D_MODEL), lambda i: (i, 0)),
            pl.BlockSpec((1, D_MODEL), lambda i: (0, 0)),
            pl.BlockSpec((None, D_MODEL, n), lambda i: (layer, 0, 0), pipeline_mode=pl.Buffered(1)),
            *extra_specs,
        ],
        out_specs=out_specs,
        out_shape=out_shape,
        scratch_shapes=scratch,
        input_output_aliases=aliases,
        compiler_params=_params("arbitrary"),
        name="proj_" + mode,
    )(x, g, w, *extra)


def _lam_from(lp):
    s1 = jnp.sum(lp[0:1, :] * lp[1:2, :], axis=-1, keepdims=True)
    s2 = jnp.sum(lp[2:3, :] * lp[3:4, :], axis=-1, keepdims=True)
    return jnp.exp(s1) - jnp.exp(s2)


def _pattn_body(lp_ref, g_ref, q_ref, k_ref, v_ref, o_ref, kb_ref, vb_ref, *, tq, lam_init):
    qi = pl.program_id(2)

    @pl.when(qi == 0)
    def _():
        kb_ref[...] = k_ref[...].astype(BF16)
        vb_ref[...] = v_ref[...].astype(BF16)

    lane = lax.broadcasted_iota(jnp.int32, (tq, V_DIM), 1)
    row = lax.broadcasted_iota(jnp.int32, (2 * tq, tq), 0)
    col = lax.broadcasted_iota(jnp.int32, (2 * tq, tq), 1)
    causal = col <= jnp.where(row >= tq, row - tq, row)
    heads = [slice(hh * V_DIM, (hh + 1) * V_DIM) for hh in range(ATTN_HEADS_PER_STEP)]

    def stacked_q(hs):
        q = q_ref[:, hs]
        zero = jnp.zeros_like(q)
        return jnp.concatenate([jnp.where(lane < HEAD_DIM, q, zero),
                                jnp.where(lane >= HEAD_DIM, q, zero)], axis=0)

    qs = [stacked_q(hs) for hs in heads]

    def block(j):
        return pl.ds(pl.multiple_of(j * tq, tq), tq)

    carry0 = []
    for q, hs in zip(qs, heads):
        s = jnp.where(causal, _dot_nt(q, kb_ref[block(qi), hs]), NEG_INF)
        m0 = jnp.max(s, axis=-1, keepdims=True)
        p = jnp.exp(s - m0)
        carry0 += [m0, jnp.sum(p, axis=-1, keepdims=True), _dot(p.astype(BF16), vb_ref[block(qi), hs])]

    def body(j, carry):
        out = []
        for n, (q, hs) in enumerate(zip(qs, heads)):
            m, l, acc = carry[3 * n:3 * n + 3]
            s = _dot_nt(q, kb_ref[block(j), hs])
            m_new = jnp.maximum(m, jnp.max(s, axis=-1, keepdims=True))
            alpha = jnp.exp(m - m_new)
            p = jnp.exp(s - m_new)
            l = alpha * l + jnp.sum(p, axis=-1, keepdims=True)
            acc = alpha * acc + _dot(p.astype(BF16), vb_ref[block(j), hs])
            out += [m_new, l, acc]
        return tuple(out)

    carry = lax.fori_loop(0, qi, body, tuple(carry0))
    lam = _lam_from(lp_ref[...]) + lam_init
    for n, hs in enumerate(heads):
        o = carry[3 * n + 2] / carry[3 * n + 1]
        o = o[:tq] - lam * o[tq:]
        o_ref[:, hs] = (_rms(o, g_ref[...]) * (1.0 - lam_init)).astype(BF16)


ATTN_HEADS_PER_STEP = 2


def _prompt_attention(q, k, v, slot, lp, g, batch, seq, lam_init):
    tq = 512
    nq = seq // tq
    width = ATTN_HEADS_PER_STEP * V_DIM
    body = functools.partial(_pattn_body, tq=tq, lam_init=lam_init)
    return pl.pallas_call(
        body,
        grid=(batch, N_HEADS // ATTN_HEADS_PER_STEP, nq),
        in_specs=[
            pl.BlockSpec((4, HEAD_DIM), lambda b, h, i: (0, 0)),
            pl.BlockSpec((1, V_DIM), lambda b, h, i: (0, 0)),
            pl.BlockSpec((tq, width), lambda b, h, i: (b * nq + i, h)),
            pl.BlockSpec((None, seq, width), lambda b, h, i: (slot, b, h)),
            pl.BlockSpec((None, seq, width), lambda b, h, i: (slot, b, h)),
        ],
        out_specs=pl.BlockSpec((tq, width), lambda b, h, i: (b * nq + i, h)),
        out_shape=jax.ShapeDtypeStruct((batch * seq, ATTN_WIDTH), BF16),
        scratch_shapes=[pltpu.VMEM((seq, width), BF16), pltpu.VMEM((seq, width), BF16)],
        compiler_params=_params("arbitrary", "arbitrary", "arbitrary"),
        name="prompt_attn",
    )(lp, g, q, k, v)


HEAD_GROUP = SUBLANES
N_GROUPS = N_HEADS // HEAD_GROUP
PAGES_PER_STEP = 8


def _dattn_body(pt_ref, lp_ref, g_ref, q_ref, kn_ref, vn_ref, *rest,
                n_steps, page, dec_seq, lam_init):
    del pt_ref
    kp_refs = rest[:PAGES_PER_STEP]
    vp_refs = rest[PAGES_PER_STEP:2 * PAGES_PER_STEP]
    o_ref, m_ref, l_ref, acc_ref, bias_ref = rest[2 * PAGES_PER_STEP:]
    step = pl.program_id(1)
    rows = 2 * dec_seq * HEAD_GROUP

    def group(ref, gi):
        x = ref[:, gi * HEAD_GROUP:(gi + 1) * HEAD_GROUP, :]
        return x.reshape(x.shape[0] * HEAD_GROUP, V_DIM).astype(BF16)

    def qg(gi):
        return q_ref[gi * rows:(gi + 1) * rows, :]

    @pl.when(step == 0)
    def _():
        r = lax.broadcasted_iota(jnp.int32, (rows, page * HEAD_GROUP), 0)
        c = lax.broadcasted_iota(jnp.int32, (rows, page * HEAD_GROUP), 1)
        same_head = (r & (HEAD_GROUP - 1)) == (c & (HEAD_GROUP - 1))
        bias_ref[...] = jnp.where(same_head, 0.0, NEG_INF)
        nk = dec_seq * HEAD_GROUP
        r2 = lax.broadcasted_iota(jnp.int32, (rows, nk), 0)
        c2 = lax.broadcasted_iota(jnp.int32, (rows, nk), 1)
        ok = ((r2 & (HEAD_GROUP - 1)) == (c2 & (HEAD_GROUP - 1))) & (
            lax.shift_right_logical(c2, 3) <= lax.shift_right_logical(r2, 4))
        for gi in range(N_GROUPS):
            sl = slice(gi * rows, (gi + 1) * rows)
            s = jnp.where(ok, _dot_nt(qg(gi), group(kn_ref, gi)), NEG_INF)
            m = jnp.max(s, axis=-1, keepdims=True)
            pr = jnp.exp(s - m)
            m_ref[sl, :] = m
            l_ref[sl, :] = jnp.sum(pr, axis=-1, keepdims=True)
            acc_ref[sl, :] = _dot(pr.astype(BF16), group(vn_ref, gi))

    bias = bias_ref[...]
    for gi in range(N_GROUPS):
        sl = slice(gi * rows, (gi + 1) * rows)
        q = qg(gi)
        ss = [_dot_nt(q, group(kp, gi)) + bias for kp in kp_refs]
        m_old = m_ref[sl, :]
        m_new = m_old
        for s in ss:
            m_new = jnp.maximum(m_new, jnp.max(s, axis=-1, keepdims=True))
        alpha = jnp.exp(m_old - m_new)
        l_new = alpha * l_ref[sl, :]
        acc = alpha * acc_ref[sl, :]
        for s, vp in zip(ss, vp_refs):
            pr = jnp.exp(s - m_new)
            l_new = l_new + jnp.sum(pr, axis=-1, keepdims=True)
            acc = acc + _dot(pr.astype(BF16), group(vp, gi))
        m_ref[sl, :] = m_new
        l_ref[sl, :] = l_new
        acc_ref[sl, :] = acc

    @pl.when(step == n_steps - 1)
    def _():
        o = acc_ref[...] / l_ref[...]
        lam = _lam_from(lp_ref[...]) + lam_init
        g = g_ref[...]
        for gi in range(N_GROUPS):
            for t in range(dec_seq):
                base = gi * rows + 2 * t * HEAD_GROUP
                o1 = o[base:base + HEAD_GROUP]
                o2 = o[base + HEAD_GROUP:base + 2 * HEAD_GROUP]
                o_ref[t, gi * HEAD_GROUP:(gi + 1) * HEAD_GROUP, :] = (
                    _rms(o1 - lam * o2, g) * (1.0 - lam_init))


def _sample_attention(qall, k_new, v_new, cache_k, cache_v, slot, page_table, lp, g, lam_init):
    db, n_pages = page_table.shape
    page = cache_k.shape[2]
    dec_seq = k_new.shape[1]
    nq = 2 * dec_seq * N_HEADS
    n_steps = n_pages // PAGES_PER_STEP
    body = functools.partial(_dattn_body, n_steps=n_steps, page=page, dec_seq=dec_seq,
                             lam_init=lam_init)

    def page_spec(i):
        return pl.BlockSpec(
            (None, None, page, N_HEADS, V_DIM),
            lambda b, s, pt: (slot, pt[b * n_pages + s * PAGES_PER_STEP + i], 0, 0, 0))

    page_specs = [page_spec(i) for i in range(PAGES_PER_STEP)]
    new_spec = pl.BlockSpec((None, dec_seq, N_HEADS, V_DIM), lambda b, s, pt: (b, 0, 0, 0))
    grid_spec = pltpu.PrefetchScalarGridSpec(
        num_scalar_prefetch=1,
        grid=(db, n_steps),
        in_specs=[
            pl.BlockSpec((4, HEAD_DIM), lambda b, s, pt: (0, 0)),
            pl.BlockSpec((1, V_DIM), lambda b, s, pt: (0, 0)),
            pl.BlockSpec((None, nq, V_DIM), lambda b, s, pt: (b, 0, 0)),
            new_spec, new_spec, *page_specs, *page_specs,
        ],
        out_specs=pl.BlockSpec((None, dec_seq, N_HEADS, V_DIM), lambda b, s, pt: (b, 0, 0, 0)),
        scratch_shapes=[
            pltpu.VMEM((nq, 1), F32), pltpu.VMEM((nq, 1), F32), pltpu.VMEM((nq, V_DIM), F32),
            pltpu.VMEM((nq // N_GROUPS, page * HEAD_GROUP), F32),
        ],
    )
    return pl.pallas_call(
        body,
        grid_spec=grid_spec,
        out_shape=jax.ShapeDtypeStruct((db, dec_seq, N_HEADS, V_DIM), F32),
        compiler_params=_params("arbitrary", "arbitrary"),
        name="sample_attn",
    )(page_table.reshape(-1), lp, g, qall, k_new, v_new,
      *([cache_k] * PAGES_PER_STEP), *([cache_v] * PAGES_PER_STEP))


def _conv_body(u_ref, buf_ref, w_ref, b_ref, uc_ref, nb_ref, pad_ref, *, t_len):
    lo = SUBLANES - (CONV_W - 1)
    pad_ref[lo:SUBLANES, :] = buf_ref[...]
    pad_ref[SUBLANES:SUBLANES + t_len, :] = u_ref[...]
    out = b_ref[...] + pad_ref[lo:lo + t_len, :] * w_ref[0:1, :]
    for j in range(1, CONV_W):
        out = out + pad_ref[lo + j:lo + j + t_len, :] * w_ref[j:j + 1, :]
    uc_ref[...] = out
    nb_ref[...] = pad_ref[t_len + lo:t_len + SUBLANES, :]


def _chan_tile(t_len):
    return D_RNN if t_len <= 2 * SUBLANES else GATE_TN


def _conv(u, buf, w, b, layer):
    bsz, t_len, _ = u.shape
    tc = _chan_tile(t_len)
    body = functools.partial(_conv_body, t_len=t_len)
    return pl.pallas_call(
        body,
        grid=(bsz, D_RNN // tc),
        in_specs=[
            pl.BlockSpec((None, t_len, tc), lambda b_, c: (b_, 0, c)),
            pl.BlockSpec((None, CONV_W - 1, tc), lambda b_, c: (b_, 0, c)),
            pl.BlockSpec((None, CONV_W, tc), lambda b_, c: (layer, 0, c)),
            pl.BlockSpec((None, 1, tc), lambda b_, c: (layer, 0, c)),
        ],
        out_specs=[
            pl.BlockSpec((None, t_len, tc), lambda b_, c: (b_, 0, c)),
            pl.BlockSpec((None, CONV_W - 1, tc), lambda b_, c: (b_, 0, c)),
        ],
        out_shape=[jax.ShapeDtypeStruct(u.shape, F32), jax.ShapeDtypeStruct(buf.shape, F32)],
        scratch_shapes=[pltpu.VMEM((t_len + SUBLANES, tc), F32)],
        compiler_params=_params("arbitrary", "arbitrary"),
        name="conv",
    )(u, buf, w, b)


def _softplus(z):
    return jnp.maximum(z, 0.0) + jnp.log1p(jnp.exp(-jnp.abs(z)))


GATE_TN = 3 * LANES
GATE_TK = 7 * LANES


def _gate_window(j):
    first_block = (GATE_TN * j) // RG_BLOCK
    k0 = (RG_BLOCK * first_block) // LANES * LANES
    smallest = min if isinstance(j, int) else jnp.minimum
    return smallest(k0, D_RNN - GATE_TK)


def _banded(w):
    n_l, n_h, bi, bj = w.shape
    w = w.astype(BF16)
    dense = jnp.concatenate(
        [jnp.pad(w[:, h], ((0, 0), (0, 0), (h * bj, (n_h - 1 - h) * bj))) for h in range(n_h)], axis=1)
    tiles = []
    for j in range(D_RNN // GATE_TN):
        k0 = _gate_window(j)
        lo_block = (GATE_TN * j) // RG_BLOCK
        hi_block = (GATE_TN * (j + 1) - 1) // RG_BLOCK
        assert k0 <= RG_BLOCK * lo_block and RG_BLOCK * (hi_block + 1) <= k0 + GATE_TK
        tiles.append(dense[:, k0:k0 + GATE_TK, GATE_TN * j:GATE_TN * (j + 1)])
    return jnp.stack(tiles, axis=1)


def _gates_body(uc_ref, wa_ref, wx_ref, ba_ref, bx_ref, lam_ref, a_ref, xin_ref, ub_ref):
    j = pl.program_id(1)

    @pl.when(j == 0)
    def _():
        ub_ref[...] = uc_ref[...].astype(BF16)

    ub = ub_ref[:, pl.ds(pl.multiple_of(_gate_window(j), LANES), GATE_TK)]
    r = jax.nn.sigmoid(_dot(ub, wa_ref[...]) + ba_ref[...])
    i = jax.nn.sigmoid(_dot(ub, wx_ref[...]) + bx_ref[...])
    log_a = -RG_C * r * _softplus(-lam_ref[...])
    a = jnp.exp(log_a)
    u = uc_ref[:, pl.ds(pl.multiple_of(j * GATE_TN, LANES), GATE_TN)]
    a_ref[...] = a
    xin_ref[...] = jnp.sqrt(1.0 - a * a) * i * u


def _gates(uc, wa_b, wx_b, ba, bx, lam, layer):
    m = uc.shape[0]
    tm = _row_tile(m, 1024)
    vec = pl.BlockSpec((None, 1, GATE_TN), lambda i, j: (layer, 0, j))
    wsp = pl.BlockSpec((None, None, GATE_TK, GATE_TN), lambda i, j: (layer, j, 0, 0))
    osp = pl.BlockSpec((tm, GATE_TN), lambda i, j: (i, j))
    return pl.pallas_call(
        _gates_body,
        grid=(m // tm, D_RNN // GATE_TN),
        in_specs=[pl.BlockSpec((tm, D_RNN), lambda i, j: (i, 0)), wsp, wsp, vec, vec, vec],
        out_specs=[osp, osp],
        out_shape=[jax.ShapeDtypeStruct((m, D_RNN), F32), jax.ShapeDtypeStruct((m, D_RNN), F32)],
        scratch_shapes=[pltpu.VMEM((tm, D_RNN), BF16)],
        compiler_params=_params("arbitrary", "arbitrary"),
        name="gates",
    )(uc, wa_b, wx_b, ba, bx, lam)


def _gates_scan_body(uc_ref, wa_ref, wx_ref, ba_ref, bx_ref, lam_ref, gate_ref, h0_ref,
                     y_ref, ht_ref, ub_ref, a_ref, xin_ref, carry_ref, *, tiles_per_seq):
    i, j = pl.program_id(0), pl.program_id(1)

    @pl.when((i == 0) & (j == 0))
    def _():
        carry_ref[...] = jnp.zeros_like(carry_ref)

    _gates_body(uc_ref, wa_ref, wx_ref, ba_ref, bx_ref, lam_ref, a_ref, xin_ref, ub_ref)
    cols = pl.ds(pl.multiple_of(j * GATE_TN, LANES), GATE_TN)
    h = jnp.where(i % tiles_per_seq == 0, h0_ref[...], carry_ref[0:1, cols])
    h = _scan_rows(a_ref, xin_ref, gate_ref, y_ref, h, a_ref.shape[0])
    carry_ref[0:1, cols] = h
    ht_ref[...] = h


def _gates_scan(uc, gate, h0, wa_b, wx_b, ba, bx, lam, layer, t_len):
    m = uc.shape[0]
    tm = _row_tile(m, 1024)
    tiles_per_seq = t_len // tm
    assert tiles_per_seq * tm == t_len
    body = functools.partial(_gates_scan_body, tiles_per_seq=tiles_per_seq)
    vec = pl.BlockSpec((None, 1, GATE_TN), lambda i, j: (layer, 0, j))
    wsp = pl.BlockSpec((None, None, GATE_TK, GATE_TN), lambda i, j: (layer, j, 0, 0))
    tile = pl.BlockSpec((tm, GATE_TN), lambda i, j: (i, j))
    hsp = pl.BlockSpec((None, 1, GATE_TN), lambda i, j: (i // tiles_per_seq, 0, j))
    y, h_tiles = pl.pallas_call(
        body,
        grid=(m // tm, D_RNN // GATE_TN),
        in_specs=[pl.BlockSpec((tm, D_RNN), lambda i, j: (i, 0)), wsp, wsp, vec, vec, vec, tile, hsp],
        out_specs=[tile, pl.BlockSpec((None, 1, GATE_TN), lambda i, j: (i, 0, j))],
        out_shape=[jax.ShapeDtypeStruct((m, D_RNN), BF16),
                   jax.ShapeDtypeStruct((m // tm, 1, D_RNN), F32)],
        scratch_shapes=[pltpu.VMEM((tm, D_RNN), BF16), pltpu.VMEM((tm, GATE_TN), F32),
                        pltpu.VMEM((tm, GATE_TN), F32), pltpu.VMEM((SUBLANES, D_RNN), F32)],
        compiler_params=_params("arbitrary", "arbitrary"),
        name="gates_scan",
    )(uc, wa_b, wx_b, ba, bx, lam, gate, h0)
    return y, h_tiles[tiles_per_seq - 1::tiles_per_seq]


def _scan_rows(a_ref, x_ref, gate_ref, y_ref, h, t_len):
    tc = a_ref.shape[-1]
    step = 2 * SUBLANES
    if t_len % step == 0:
        row = lax.broadcasted_iota(jnp.int32, (SUBLANES, tc), 0)

        def chunk(sl, h):
            a = a_ref[sl, :]
            x = x_ref[sl, :]
            d = 1
            while d < SUBLANES:
                valid = row >= d
                x = jnp.where(valid, a * pltpu.roll(x, d, 0) + x, x)
                a = jnp.where(valid, a * pltpu.roll(a, d, 0), a)
                d *= 2
            return a * h + x

        def body(c, h):
            base = pl.multiple_of(c * step, step)
            hs1 = chunk(pl.ds(base, SUBLANES), h)
            hs2 = chunk(pl.ds(pl.multiple_of(base + SUBLANES, SUBLANES), SUBLANES),
                        hs1[SUBLANES - 1:SUBLANES, :])
            sl = pl.ds(base, step)
            y_ref[sl, :] = (jnp.concatenate([hs1, hs2], axis=0) * gate_ref[sl, :]).astype(y_ref.dtype)
            return hs2[SUBLANES - 1:SUBLANES, :]

        h = lax.fori_loop(0, t_len // step, body, h)
    else:
        for t in range(t_len):
            h = a_ref[t:t + 1, :] * h + x_ref[t:t + 1, :]
            y_ref[t:t + 1, :] = (h * gate_ref[t:t + 1, :]).astype(y_ref.dtype)
    return h


def _scan_body(a_ref, x_ref, gate_ref, h0_ref, y_ref, ht_ref, *, t_len):
    ht_ref[...] = _scan_rows(a_ref, x_ref, gate_ref, y_ref, h0_ref[...], t_len)


def _scan(a, xin, gate, h0):
    bsz, t_len, _ = a.shape
    tc = _chan_tile(t_len)
    body = functools.partial(_scan_body, t_len=t_len)
    seq_spec = pl.BlockSpec((None, t_len, tc), lambda b_, c: (b_, 0, c))
    h_spec = pl.BlockSpec((None, 1, tc), lambda b_, c: (b_, 0, c))
    y_dtype = BF16 if t_len % (2 * SUBLANES) == 0 else F32
    return pl.pallas_call(
        body,
        grid=(bsz, D_RNN // tc),
        in_specs=[seq_spec, seq_spec, seq_spec, h_spec],
        out_specs=[seq_spec, h_spec],
        out_shape=[jax.ShapeDtypeStruct(a.shape, y_dtype), jax.ShapeDtypeStruct((bsz, 1, D_RNN), F32)],
        compiler_params=_params("arbitrary", "arbitrary"),
        name="scan",
    )(a, xin, gate, h0)


def _rope_tables(pos):
    half = ROT_DIM // 2
    inv = ROPE_THETA ** (-jnp.arange(0, ROT_DIM, 2, dtype=F32) / ROT_DIM)
    ang = pos.astype(F32)[:, None] * inv[None, :]
    cos, sin = jnp.cos(ang), jnp.sin(ang)
    t_len = pos.shape[0]
    ones = jnp.ones((t_len, HEAD_DIM - ROT_DIM), F32)
    zer = jnp.zeros((t_len, HEAD_DIM - ROT_DIM), F32)
    zh = jnp.zeros((t_len, half), F32)
    c = jnp.concatenate([cos, cos, ones], axis=1)
    sa = jnp.concatenate([-sin, zh, zer], axis=1)
    sb = jnp.concatenate([zh, sin, zer], axis=1)
    return tuple(jnp.concatenate([t, t], axis=1) for t in (c, sa, sb))


def _trunk(x, bsz, t_len, p, ffn_w, tabs, period_rows, attn_fn, h0, buf0):
    hs, bufs, emitted = [], [], []
    n_attn = (DEPTH + 1) // 2
    ks = jnp.zeros((n_attn, x.shape[0], ATTN_WIDTH), F32)
    vs = jnp.zeros((n_attn, x.shape[0], ATTN_WIDTH), F32)

    def ffn(x, g, which, l, gf=None):
        if ffn_w is not None:
            return _ffn(x, g, *ffn_w[l][which], gf=gf), None
        names = ("pre_w1", "pre_w3", "pre_w2") if which == 0 else ("post_w1", "post_w3", "post_w2")
        y, *wb = _ffn(x, g, *(p[n] for n in names), gf=gf, layer=l)
        return y, tuple(wb)

    for l in range(DEPTH):
        j = l // 2
        x, wb_pre = ffn(x, p["ln_ffn_pre"][l:l + 1], 0, l)
        g_mix = p["ln_mix"][l:l + 1]
        if l % 2 == 0:
            lam_init = 0.8 - 0.6 * math.exp(-0.3 * l)
            q = _proj(x, g_mix, p["wq"], j, "q", tabs=tabs, period_rows=period_rows)
            ks = _proj(x, g_mix, p["wk"], j, "k", tabs=tabs, period_rows=period_rows,
                       stack=(n_attn, j, ks))
            vs = _proj(x, g_mix, p["wv"], j, "v", stack=(n_attn, j, vs))
            o = attn_fn(j, q, ks, vs, p["lp"][j], p["attn_subln"][j:j + 1], lam_init)
            x = _mm_res(o, p["wo"], j, x)
        else:
            gate = _proj(x, g_mix, p["w_gate"], j, "gelu", bias=p["b_gate"])
            if t_len % _proj_row_tile(bsz * t_len, D_RNN) == 0:
                uc, nb = _proj(x, g_mix, p["w_in"], j, "conv", bias=p["b_in"],
                               conv=(p["conv_w"], p["conv_b"], buf0[j], t_len))
            else:
                u = _proj(x, g_mix, p["w_in"], j, "bias", bias=p["b_in"])
                uc, nb = _conv(u.reshape(bsz, t_len, D_RNN), buf0[j], p["conv_w"], p["conv_b"], j)
            gate_w = (p["wa_b"], p["wx_b"], p["ba"], p["bx"], p["lam"], j)
            uc = uc.reshape(bsz * t_len, D_RNN)
            if t_len % _row_tile(bsz * t_len, 1024) == 0:
                yg, ht = _gates_scan(uc, gate, h0[j][:, None, :], *gate_w, t_len)
            else:
                a, xin = _gates(uc, *gate_w)
                yg, ht = _scan(a.reshape(bsz, t_len, D_RNN), xin.reshape(bsz, t_len, D_RNN),
                               gate.reshape(bsz, t_len, D_RNN), h0[j][:, None, :])
            x = _mm_res(yg.reshape(bsz * t_len, D_RNN), p["w_out"], j, x)
            hs.append(ht[:, 0, :])
            bufs.append(nb)
        gf = p["ln_final"] if l == DEPTH - 1 else None
        x, wb_post = ffn(x, p["ln_ffn_post"][l:l + 1], 1, l, gf)
        emitted.append((wb_pre, wb_post))
    return x, ks, vs, hs, bufs, emitted


def kernel(x_prompt, x_sample, cache_k, cache_v, state_h, state_conv, page_table, ln_ffn_pre, ffn_pre_w1, ffn_pre_w3, ffn_pre_w2, ln_mix, ln_ffn_post, ffn_post_w1, ffn_post_w3, ffn_post_w2, attn_wq, attn_wk, attn_wv, attn_lq1, attn_lk1, attn_lq2, attn_lk2, attn_subln, attn_wo, rec_w_gate, rec_b_gate, rec_w_in, rec_b_in, rec_conv_w, rec_conv_b, rec_wa, rec_ba, rec_wx, rec_bx, rec_lam, rec_w_out, ln_final):
    n_b = rec_lam.shape[0]
    bsz, seq = x_prompt.shape[0], x_prompt.shape[1]
    db, dec_seq = x_sample.shape[0], x_sample.shape[1]
    past = page_table.shape[1] * cache_k.shape[2]

    p = {
        "ln_ffn_pre": ln_ffn_pre, "ln_mix": ln_mix, "ln_ffn_post": ln_ffn_post,
        "ln_final": ln_final[None, :],
        "pre_w1": ffn_pre_w1, "pre_w3": ffn_pre_w3, "pre_w2": ffn_pre_w2,
        "post_w1": ffn_post_w1, "post_w3": ffn_post_w3, "post_w2": ffn_post_w2,
        "wq": attn_wq.astype(BF16), "wk": attn_wk.astype(BF16), "wv": attn_wv.astype(BF16),
        "lp": jnp.stack([attn_lq1, attn_lk1, attn_lq2, attn_lk2], axis=1).astype(F32),
        "attn_subln": attn_subln, "wo": attn_wo.astype(BF16),
        "w_gate": rec_w_gate.astype(BF16), "w_in": rec_w_in.astype(BF16),
        "b_gate": rec_b_gate[:, None, :], "b_in": rec_b_in[:, None, :],
        "conv_w": rec_conv_w, "conv_b": rec_conv_b[:, None, :],
        "wa_b": _banded(rec_wa), "wx_b": _banded(rec_wx),
        "ba": rec_ba[:, None, :], "bx": rec_bx[:, None, :], "lam": rec_lam[:, None, :],
        "w_out": rec_w_out.astype(BF16),
    }

    def sample_mix(j, q, k, v, lp, g, lam_init):
        q5 = q.reshape(db, dec_seq, N_HEADS, 2, HEAD_DIM)
        zero = jnp.zeros_like(q5[:, :, :, 0])
        q1 = jnp.concatenate([q5[:, :, :, 0], zero], axis=-1)
        q2 = jnp.concatenate([zero, q5[:, :, :, 1]], axis=-1)
        qall = jnp.stack([q1, q2], axis=2).reshape(db, dec_seq, 2, N_GROUPS, HEAD_GROUP, V_DIM)
        qall = qall.transpose(0, 3, 1, 2, 4, 5).reshape(db, dec_seq * 2 * N_HEADS, V_DIM)
        o = _sample_attention(qall, k[j].reshape(db, dec_seq, N_HEADS, V_DIM),
                              v[j].reshape(db, dec_seq, N_HEADS, V_DIM), cache_k, cache_v, j,
                              page_table, lp, g, lam_init)
        return o.reshape(db * dec_seq, ATTN_WIDTH)

    pos_s = past + jnp.tile(jnp.arange(dec_seq), db)
    y_s, k_s, v_s, h_s, c_s, ffn_w = _trunk(
        x_sample.reshape(db * dec_seq, D_MODEL), db, dec_seq, p, None,
        _rope_tables(pos_s), db * dec_seq, sample_mix, state_h, state_conv)

    def prompt_mix(j, q, k, v, lp, g, lam_init):
        return _prompt_attention(q, k, v, j, lp, g, bsz, seq, lam_init)

    h0_p = jnp.zeros((n_b, bsz, D_RNN), F32)
    buf0_p = jnp.zeros((n_b, bsz, CONV_W - 1, D_RNN), F32)
    y_p, k_p, v_p, h_p, c_p, _ = _trunk(
        x_prompt.reshape(bsz * seq, D_MODEL), bsz, seq, p, ffn_w,
        _rope_tables(jnp.arange(seq)), seq, prompt_mix, h0_p, buf0_p)

    def kv(xs, b_, t_):
        return xs.reshape(xs.shape[0], b_, t_, N_HEADS, V_DIM)

    return (y_p.reshape(bsz, seq, D_MODEL), y_s.reshape(db, dec_seq, D_MODEL),
            kv(k_p, bsz, seq), kv(v_p, bsz, seq), jnp.stack(h_p), jnp.stack(c_p),
            kv(k_s, db, dec_seq), kv(v_s, db, dec_seq), jnp.stack(h_s), jnp.stack(c_s))
```

```python
import functools
import math

import jax
import jax.numpy as jnp
from jax import lax
from jax.experimental import pallas as pl
from jax.experimental.pallas import tpu as pltpu

F32 = jnp.float32
BF16 = jnp.bfloat16

D_MODEL = 2048
DEPTH = 4
N_HEADS = 16
HEAD_DIM = 64
V_DIM = 2 * HEAD_DIM
ATTN_WIDTH = N_HEADS * V_DIM
ROT_DIM = HEAD_DIM // 4
ROPE_THETA = 500000.0
D_RNN = 2688
RG_HEADS = 16
RG_BLOCK = D_RNN // RG_HEADS
CONV_W = 4
RG_C = 8.0
D_FF = 5632
EPS = 1e-6

LANES = 128
SUBLANES = 8
VMEM_LIMIT_BYTES = 60 * 1024 * 1024
NEG_INF = float("-inf")


def _params(*sem):
    return pltpu.CompilerParams(dimension_semantics=sem, vmem_limit_bytes=VMEM_LIMIT_BYTES)


def _rms(x, g):
    ms = jnp.mean(x * x, axis=-1, keepdims=True)
    return x * lax.rsqrt(ms + EPS) * g


def _dot(a, b):
    return jnp.dot(a, b, preferred_element_type=F32)


def _dot_nt(a, b):
    return lax.dot_general(a, b, (((1,), (1,)), ((), ())), preferred_element_type=F32)


def _row_tile(m, cap):
    return cap if m % cap == 0 else m


def _ffn_body(x_ref, g_ref, w1_ref, w3_ref, w2_ref, gf_ref, o_ref, *rest, n_f, final_norm, emit):
    j = pl.program_id(1)
    xn_ref = rest[-1]

    @pl.when(j == 0)
    def _():
        xn_ref[...] = _rms(x_ref[...], g_ref[...]).astype(BF16)
        o_ref[...] = jnp.zeros_like(o_ref)

    w1, w3, w2 = w1_ref[...], w3_ref[...], w2_ref[...]
    if emit:
        w1, w3, w2 = w1.astype(BF16), w3.astype(BF16), w2.astype(BF16)
        rest[0][...], rest[1][...], rest[2][...] = w1, w3, w2
    xn = xn_ref[...]
    a = _dot(xn, w1)
    b = _dot(xn, w3)
    h = (jax.nn.silu(a) * b).astype(BF16)
    o_ref[...] += _dot(h, w2)

    @pl.when(j == n_f - 1)
    def _():
        y = x_ref[...] + 0.5 * o_ref[...]
        if final_norm:
            y = _rms(y, gf_ref[...])
        o_ref[...] = y


def _ffn(x, g, w1, w3, w2, gf=None, layer=None):
    m = x.shape[0]
    tm = _row_tile(m, 1024)
    tf = 512
    n_f = D_FF // tf
    emit = layer is not None
    assert not emit or m == tm
    final_norm = gf is not None
    if gf is None:
        gf = g
    body = functools.partial(_ffn_body, n_f=n_f, final_norm=final_norm, emit=emit)
    up_spec = pl.BlockSpec((D_MODEL, tf), lambda i, j: (0, j))
    down_spec = pl.BlockSpec((tf, D_MODEL), lambda i, j: (j, 0))
    out_specs = [pl.BlockSpec((tm, D_MODEL), lambda i, j: (i, 0))]
    out_shape = [jax.ShapeDtypeStruct((m, D_MODEL), F32)]
    w_specs = [up_spec, up_spec, down_spec]
    if emit:
        out_specs += w_specs
        out_shape += [jax.ShapeDtypeStruct(w.shape[1:], BF16) for w in (w1, w3, w2)]
        w_specs = [pl.BlockSpec((None, D_MODEL, tf), lambda i, j: (layer, 0, j)),
                   pl.BlockSpec((None, D_MODEL, tf), lambda i, j: (layer, 0, j)),
                   pl.BlockSpec((None, tf, D_MODEL), lambda i, j: (layer, j, 0))]
    outs = pl.pallas_call(
        body,
        grid=(m // tm, n_f),
        in_specs=[
            pl.BlockSpec((tm, D_MODEL), lambda i, j: (i, 0),
                         pipeline_mode=pl.Buffered(1 if final_norm else 2)),
            pl.BlockSpec((1, D_MODEL), lambda i, j: (0, 0)),
            *w_specs,
            pl.BlockSpec((1, D_MODEL), lambda i, j: (0, 0)),
        ],
        out_specs=out_specs,
        out_shape=out_shape,
        scratch_shapes=[pltpu.VMEM((tm, D_MODEL), BF16)],
        compiler_params=_params("arbitrary", "arbitrary"),
        name="ffn_cast" if emit else "ffn",
    )(x, g, w1, w3, w2, gf)
    return outs if emit else outs[0]


def _mm_res_body(a_ref, w_ref, r_ref, o_ref, *scratch):
    if scratch:
        ab_ref, = scratch

        @pl.when(pl.program_id(1) == 0)
        def _():
            ab_ref[...] = a_ref[...].astype(BF16)
    else:
        ab_ref = a_ref

    o_ref[...] = r_ref[...] + _dot(ab_ref[...], w_ref[...])


def _mm_res(a, w, layer, resid):
    m, k = a.shape
    n = w.shape[-1]
    tm = _row_tile(m, 1024)
    tn = 1024
    scratch = [] if a.dtype == BF16 else [pltpu.VMEM((tm, k), BF16)]
    return pl.pallas_call(
        _mm_res_body,
        grid=(m // tm, n // tn),
        in_specs=[
            pl.BlockSpec((tm, k), lambda i, j: (i, 0)),
            pl.BlockSpec((None, k, tn), lambda i, j: (layer, 0, j)),
            pl.BlockSpec((tm, tn), lambda i, j: (i, j)),
        ],
        out_specs=pl.BlockSpec((tm, tn), lambda i, j: (i, j)),
        out_shape=jax.ShapeDtypeStruct((m, n), F32),
        scratch_shapes=scratch,
        compiler_params=_params("arbitrary", "arbitrary"),
        name="mm_res",
    )(a, w, resid)


def _proj_conv_body(x_ref, g_ref, w_ref, b_ref, cw_ref, cb_ref, buf_ref, o_ref, nb_ref,
                    carry_ref, pad_ref, *, chunk, tiles_per_seq):
    i = pl.program_id(0)
    tm = o_ref.shape[0]
    lo = SUBLANES - (CONV_W - 1)

    @pl.when(i == 0)
    def _():
        carry_ref[...] = jnp.zeros_like(carry_ref)

    first = (i % tiles_per_seq) == 0
    xn = _rms(x_ref[...], g_ref[...]).astype(BF16)
    for c0 in range(0, o_ref.shape[-1], chunk):
        cols = slice(c0, c0 + chunk)
        u = _dot(xn, w_ref[:, cols]) + b_ref[:, cols]
        pad_ref[lo:SUBLANES, :] = jnp.where(first, buf_ref[:, cols], carry_ref[0:CONV_W - 1, cols])
        pad_ref[SUBLANES:SUBLANES + tm, :] = u
        out = cb_ref[:, cols] + pad_ref[lo:lo + tm, :] * cw_ref[0:1, cols]
        for j in range(1, CONV_W):
            out = out + pad_ref[lo + j:lo + j + tm, :] * cw_ref[j:j + 1, cols]
        o_ref[:, cols] = out
        last = pad_ref[tm + lo:tm + SUBLANES, :]
        carry_ref[0:CONV_W - 1, cols] = last
        nb_ref[:, cols] = last


def _proj_body(x_ref, g_ref, w_ref, *rest, mode, chunk, conv_args=None):
    if mode == "conv":
        _proj_conv_body(x_ref, g_ref, w_ref, *rest, chunk=chunk, **conv_args)
        return
    o_ref = rest[-1]
    xn = _rms(x_ref[...], g_ref[...]).astype(BF16)
    for c0 in range(0, o_ref.shape[-1], chunk):
        y = _dot(xn, w_ref[:, c0:c0 + chunk])
        if mode in ("q", "k"):
            c, sa, sb = rest[0][...], rest[1][...], rest[2][...]
            parts = []
            for l0 in range(0, chunk, LANES):
                yc = y[:, l0:l0 + LANES]
                parts.append(yc * c + pltpu.roll(yc, LANES - ROT_DIM // 2, 1) * sa
                             + pltpu.roll(yc, ROT_DIM // 2, 1) * sb)
            y = jnp.concatenate(parts, axis=1)
            if mode == "q":
                y = y * (HEAD_DIM ** -0.5)
        elif mode in ("bias", "gelu"):
            y = y + rest[0][:, c0:c0 + chunk]
            if mode == "gelu":
                y = jax.nn.gelu(y)
        o_ref[:, c0:c0 + chunk] = y.astype(o_ref.dtype)


def _proj_row_tile(m, n):
    return _row_tile(m, 1024 if n <= D_MODEL else 512)


def _proj(x, g, w, layer, mode, *, tabs=None, period_rows=None, bias=None, conv=None, stack=None):
    m = x.shape[0]
    n = w.shape[-1]
    tm = _proj_row_tile(m, n)
    chunk = 512 if n % 512 == 0 else n // 3
    extra, extra_specs, scratch, conv_args = [], [], [], None
    out_specs = pl.BlockSpec((tm, n), lambda i: (i, 0))
    out_shape = jax.ShapeDtypeStruct((m, n), BF16 if mode == "q" else F32)
    vec_spec = pl.BlockSpec((None, 1, n), lambda i: (layer, 0, 0))
    if mode in ("q", "k"):
        pb = period_rows // tm
        extra = list(tabs)
        extra_specs = [pl.BlockSpec((tm, LANES), lambda i: (i % pb, 0))] * 3
    elif mode in ("bias", "gelu"):
        extra = [bias]
        extra_specs = [vec_spec]
    elif mode == "conv":
        cw, cb, buf, t_len = conv
        tiles_per_seq = t_len // tm
        assert tiles_per_seq * tm == t_len
        conv_args = dict(tiles_per_seq=tiles_per_seq)
        seq_spec = pl.BlockSpec((None, CONV_W - 1, n), lambda i: (i // tiles_per_seq, 0, 0))
        extra = [bias, cw, cb, buf]
        extra_specs = [vec_spec, pl.BlockSpec((None, CONV_W, n), lambda i: (layer, 0, 0)),
                       vec_spec, seq_spec]
        out_specs = [out_specs, seq_spec]
        out_shape = [out_shape, jax.ShapeDtypeStruct(buf.shape, F32)]
        scratch = [pltpu.VMEM((SUBLANES, n), F32), pltpu.VMEM((tm + SUBLANES, chunk), F32)]
    aliases = {}
    if stack is not None:
        n_slots, slot, prev = stack
        out_specs = pl.BlockSpec((None, tm, n), lambda i: (slot, i, 0))
        out_shape = jax.ShapeDtypeStruct((n_slots, m, n), out_shape.dtype)
        if prev is not None:
            aliases = {3 + len(extra): 0}
            extra = extra + [prev]
            extra_specs = extra_specs + [pl.BlockSpec(memory_space=pl.ANY)]
    body = functools.partial(_proj_body, mode=mode, chunk=chunk, conv_args=conv_args)
    return pl.pallas_call(
        body,
        grid=(m // tm,),
        in_specs=[
            pl.BlockSpec((tm, D_MODEL), lambda i: (i, 0)),
            pl.BlockSpec((1, D_MODEL), lambda i: (0, 0)),
            pl.BlockSpec((None, D_MODEL, n), lambda i: (layer, 0, 0), pipeline_mode=pl.Buffered(1)),
            *extra_specs,
        ],
        out_specs=out_specs,
        out_shape=out_shape,
        scratch_shapes=scratch,
        input_output_aliases=aliases,
        compiler_params=_params("arbitrary"),
        name="proj_" + mode,
    )(x, g, w, *extra)


def _lam_from(lp):
    s1 = jnp.sum(lp[0:1, :] * lp[1:2, :], axis=-1, keepdims=True)
    s2 = jnp.sum(lp[2:3, :] * lp[3:4, :], axis=-1, keepdims=True)
    return jnp.exp(s1) - jnp.exp(s2)


def _pattn_body(lp_ref, g_ref, q_ref, k_ref, v_ref, o_ref, kb_ref, vb_ref, *, tq, lam_init):
    qi = pl.program_id(2)

    @pl.when(qi == 0)
    def _():
        kb_ref[...] = k_ref[...].astype(BF16)
        vb_ref[...] = v_ref[...].astype(BF16)

    lane = lax.broadcasted_iota(jnp.int32, (tq, V_DIM), 1)
    row = lax.broadcasted_iota(jnp.int32, (2 * tq, tq), 0)
    col = lax.broadcasted_iota(jnp.int32, (2 * tq, tq), 1)
    causal = col <= jnp.where(row >= tq, row - tq, row)
    heads = [slice(hh * V_DIM, (hh + 1) * V_DIM) for hh in range(ATTN_HEADS_PER_STEP)]

    def stacked_q(hs):
        q = q_ref[:, hs]
        zero = jnp.zeros_like(q)
        return jnp.concatenate([jnp.where(lane < HEAD_DIM, q, zero),
                                jnp.where(lane >= HEAD_DIM, q, zero)], axis=0)

    qs = [stacked_q(hs) for hs in heads]

    def block(j):
        return pl.ds(pl.multiple_of(j * tq, tq), tq)

    carry0 = []
    for q, hs in zip(qs, heads):
        s = jnp.where(causal, _dot_nt(q, kb_ref[block(qi), hs]), NEG_INF)
        m0 = jnp.max(s, axis=-1, keepdims=True)
        p = jnp.exp(s - m0)
        carry0 += [m0, jnp.sum(p, axis=-1, keepdims=True), _dot(p.astype(BF16), vb_ref[block(qi), hs])]

    def body(j, carry):
        out = []
        for n, (q, hs) in enumerate(zip(qs, heads)):
            m, l, acc = carry[3 * n:3 * n + 3]
            s = _dot_nt(q, kb_ref[block(j), hs])
            m_new = jnp.maximum(m, jnp.max(s, axis=-1, keepdims=True))
            alpha = jnp.exp(m - m_new)
            p = jnp.exp(s - m_new)
            l = alpha * l + jnp.sum(p, axis=-1, keepdims=True)
            acc = alpha * acc + _dot(p.astype(BF16), vb_ref[block(j), hs])
            out += [m_new, l, acc]
        return tuple(out)

    carry = lax.fori_loop(0, qi, body, tuple(carry0))
    lam = _lam_from(lp_ref[...]) + lam_init
    for n, hs in enumerate(heads):
        o = carry[3 * n + 2] / carry[3 * n + 1]
        o = o[:tq] - lam * o[tq:]
        o_ref[:, hs] = (_rms(o, g_ref[...]) * (1.0 - lam_init)).astype(BF16)


ATTN_HEADS_PER_STEP = 4


def _prompt_attention(q, k, v, slot, lp, g, batch, seq, lam_init):
    tq = 512
    nq = seq // tq
    width = ATTN_HEADS_PER_STEP * V_DIM
    body = functools.partial(_pattn_body, tq=tq, lam_init=lam_init)
    return pl.pallas_call(
        body,
        grid=(batch, N_HEADS // ATTN_HEADS_PER_STEP, nq),
        in_specs=[
            pl.BlockSpec((4, HEAD_DIM), lambda b, h, i: (0, 0)),
            pl.BlockSpec((1, V_DIM), lambda b, h, i: (0, 0)),
            pl.BlockSpec((tq, width), lambda b, h, i: (b * nq + i, h)),
            pl.BlockSpec((None, seq, width), lambda b, h, i: (slot, b, h)),
            pl.BlockSpec((None, seq, width), lambda b, h, i: (slot, b, h)),
        ],
        out_specs=pl.BlockSpec((tq, width), lambda b, h, i: (b * nq + i, h)),
        out_shape=jax.ShapeDtypeStruct((batch * seq, ATTN_WIDTH), BF16),
        scratch_shapes=[pltpu.VMEM((seq, width), BF16), pltpu.VMEM((seq, width), BF16)],
        compiler_params=_params("arbitrary", "arbitrary", "arbitrary"),
        name="prompt_attn",
    )(lp, g, q, k, v)


HEAD_GROUP = SUBLANES
N_GROUPS = N_HEADS // HEAD_GROUP
PAGES_PER_STEP = 8


def _dattn_body(pt_ref, lp_ref, g_ref, q_ref, kn_ref, vn_ref, *rest,
                n_steps, page, dec_seq, lam_init):
    del pt_ref
    kp_refs = rest[:PAGES_PER_STEP]
    vp_refs = rest[PAGES_PER_STEP:2 * PAGES_PER_STEP]
    o_ref, m_ref, l_ref, acc_ref, bias_ref = rest[2 * PAGES_PER_STEP:]
    step = pl.program_id(1)
    rows = 2 * dec_seq * HEAD_GROUP

    def group(ref, gi):
        x = ref[:, gi * HEAD_GROUP:(gi + 1) * HEAD_GROUP, :]
        return x.reshape(x.shape[0] * HEAD_GROUP, V_DIM).astype(BF16)

    def qg(gi):
        return q_ref[gi * rows:(gi + 1) * rows, :]

    @pl.when(step == 0)
    def _():
        r = lax.broadcasted_iota(jnp.int32, (rows, page * HEAD_GROUP), 0)
        c = lax.broadcasted_iota(jnp.int32, (rows, page * HEAD_GROUP), 1)
        same_head = (r & (HEAD_GROUP - 1)) == (c & (HEAD_GROUP - 1))
        bias_ref[...] = jnp.where(same_head, 0.0, NEG_INF)
        nk = dec_seq * HEAD_GROUP
        r2 = lax.broadcasted_iota(jnp.int32, (rows, nk), 0)
        c2 = lax.broadcasted_iota(jnp.int32, (rows, nk), 1)
        ok = ((r2 & (HEAD_GROUP - 1)) == (c2 & (HEAD_GROUP - 1))) & (
            lax.shift_right_logical(c2, 3) <= lax.shift_right_logical(r2, 4))
        for gi in range(N_GROUPS):
            sl = slice(gi * rows, (gi + 1) * rows)
            s = jnp.where(ok, _dot_nt(qg(gi), group(kn_ref, gi)), NEG_INF)
            m = jnp.max(s, axis=-1, keepdims=True)
            pr = jnp.exp(s - m)
            m_ref[sl, :] = m
            l_ref[sl, :] = jnp.sum(pr, axis=-1, keepdims=True)
            acc_ref[sl, :] = _dot(pr.astype(BF16), group(vn_ref, gi))

    bias = bias_ref[...]
    for gi in range(N_GROUPS):
        sl = slice(gi * rows, (gi + 1) * rows)
        q = qg(gi)
        ss = [_dot_nt(q, group(kp, gi)) + bias for kp in kp_refs]
        m_old = m_ref[sl, :]
        m_new = m_old
        for s in ss:
            m_new = jnp.maximum(m_new, jnp.max(s, axis=-1, keepdims=True))
        alpha = jnp.exp(m_old - m_new)
        l_new = alpha * l_ref[sl, :]
        acc = alpha * acc_ref[sl, :]
        for s, vp in zip(ss, vp_refs):
            pr = jnp.exp(s - m_new)
            l_new = l_new + jnp.sum(pr, axis=-1, keepdims=True)
            acc = acc + _dot(pr.astype(BF16), group(vp, gi))
        m_ref[sl, :] = m_new
        l_ref[sl, :] = l_new
        acc_ref[sl, :] = acc

    @pl.when(step == n_steps - 1)
    def _():
        o = acc_ref[...] / l_ref[...]
        lam = _lam_from(lp_ref[...]) + lam_init
        g = g_ref[...]
        for gi in range(N_GROUPS):
            for t in range(dec_seq):
                base = gi * rows + 2 * t * HEAD_GROUP
                o1 = o[base:base + HEAD_GROUP]
                o2 = o[base + HEAD_GROUP:base + 2 * HEAD_GROUP]
                o_ref[t, gi * HEAD_GROUP:(gi + 1) * HEAD_GROUP, :] = (
                    _rms(o1 - lam * o2, g) * (1.0 - lam_init))


def _sample_attention(qall, k_new, v_new, cache_k, cache_v, slot, page_table, lp, g, lam_init):
    db, n_pages = page_table.shape
    page = cache_k.shape[2]
    dec_seq = k_new.shape[1]
    nq = 2 * dec_seq * N_HEADS
    n_steps = n_pages // PAGES_PER_STEP
    body = functools.partial(_dattn_body, n_steps=n_steps, page=page, dec_seq=dec_seq,
                             lam_init=lam_init)

    def page_spec(i):
        return pl.BlockSpec(
            (None, None, page, N_HEADS, V_DIM),
            lambda b, s, pt: (slot, pt[b * n_pages + s * PAGES_PER_STEP + i], 0, 0, 0))

    page_specs = [page_spec(i) for i in range(PAGES_PER_STEP)]
    new_spec = pl.BlockSpec((None, dec_seq, N_HEADS, V_DIM), lambda b, s, pt: (b, 0, 0, 0))
    grid_spec = pltpu.PrefetchScalarGridSpec(
        num_scalar_prefetch=1,
        grid=(db, n_steps),
        in_specs=[
            pl.BlockSpec((4, HEAD_DIM), lambda b, s, pt: (0, 0)),
            pl.BlockSpec((1, V_DIM), lambda b, s, pt: (0, 0)),
            pl.BlockSpec((None, nq, V_DIM), lambda b, s, pt: (b, 0, 0)),
            new_spec, new_spec, *page_specs, *page_specs,
        ],
        out_specs=pl.BlockSpec((None, dec_seq, N_HEADS, V_DIM), lambda b, s, pt: (b, 0, 0, 0)),
        scratch_shapes=[
            pltpu.VMEM((nq, 1), F32), pltpu.VMEM((nq, 1), F32), pltpu.VMEM((nq, V_DIM), F32),
            pltpu.VMEM((nq // N_GROUPS, page * HEAD_GROUP), F32),
        ],
    )
    return pl.pallas_call(
        body,
        grid_spec=grid_spec,
        out_shape=jax.ShapeDtypeStruct((db, dec_seq, N_HEADS, V_DIM), F32),
        compiler_params=_params("arbitrary", "arbitrary"),
        name="sample_attn",
    )(page_table.reshape(-1), lp, g, qall, k_new, v_new,
      *([cache_k] * PAGES_PER_STEP), *([cache_v] * PAGES_PER_STEP))


def _conv_body(u_ref, buf_ref, w_ref, b_ref, uc_ref, nb_ref, pad_ref, *, t_len):
    lo = SUBLANES - (CONV_W - 1)
    pad_ref[lo:SUBLANES, :] = buf_ref[...]
    pad_ref[SUBLANES:SUBLANES + t_len, :] = u_ref[...]
    out = b_ref[...] + pad_ref[lo:lo + t_len, :] * w_ref[0:1, :]
    for j in range(1, CONV_W):
        out = out + pad_ref[lo + j:lo + j + t_len, :] * w_ref[j:j + 1, :]
    uc_ref[...] = out
    nb_ref[...] = pad_ref[t_len + lo:t_len + SUBLANES, :]


def _chan_tile(t_len):
    return D_RNN if t_len <= 2 * SUBLANES else GATE_TN


def _conv(u, buf, w, b, layer):
    bsz, t_len, _ = u.shape
    tc = _chan_tile(t_len)
    body = functools.partial(_conv_body, t_len=t_len)
    return pl.pallas_call(
        body,
        grid=(bsz, D_RNN // tc),
        in_specs=[
            pl.BlockSpec((None, t_len, tc), lambda b_, c: (b_, 0, c)),
            pl.BlockSpec((None, CONV_W - 1, tc), lambda b_, c: (b_, 0, c)),
            pl.BlockSpec((None, CONV_W, tc), lambda b_, c: (layer, 0, c)),
            pl.BlockSpec((None, 1, tc), lambda b_, c: (layer, 0, c)),
        ],
        out_specs=[
            pl.BlockSpec((None, t_len, tc), lambda b_, c: (b_, 0, c)),
            pl.BlockSpec((None, CONV_W - 1, tc), lambda b_, c: (b_, 0, c)),
        ],
        out_shape=[jax.ShapeDtypeStruct(u.shape, F32), jax.ShapeDtypeStruct(buf.shape, F32)],
        scratch_shapes=[pltpu.VMEM((t_len + SUBLANES, tc), F32)],
        compiler_params=_params("arbitrary", "arbitrary"),
        name="conv",
    )(u, buf, w, b)


def _softplus(z):
    return jnp.maximum(z, 0.0) + jnp.log1p(jnp.exp(-jnp.abs(z)))


GATE_TN = 3 * LANES
GATE_TK = 7 * LANES


def _gate_window(j):
    first_block = (GATE_TN * j) // RG_BLOCK
    k0 = (RG_BLOCK * first_block) // LANES * LANES
    smallest = min if isinstance(j, int) else jnp.minimum
    return smallest(k0, D_RNN - GATE_TK)


def _banded(w):
    n_l, n_h, bi, bj = w.shape
    w = w.astype(BF16)
    dense = jnp.concatenate(
        [jnp.pad(w[:, h], ((0, 0), (0, 0), (h * bj, (n_h - 1 - h) * bj))) for h in range(n_h)], axis=1)
    tiles = []
    for j in range(D_RNN // GATE_TN):
        k0 = _gate_window(j)
        lo_block = (GATE_TN * j) // RG_BLOCK
        hi_block = (GATE_TN * (j + 1) - 1) // RG_BLOCK
        assert k0 <= RG_BLOCK * lo_block and RG_BLOCK * (hi_block + 1) <= k0 + GATE_TK
        tiles.append(dense[:, k0:k0 + GATE_TK, GATE_TN * j:GATE_TN * (j + 1)])
    return jnp.stack(tiles, axis=1)


def _gates_body(uc_ref, wa_ref, wx_ref, ba_ref, bx_ref, lam_ref, a_ref, xin_ref, ub_ref):
    j = pl.program_id(1)

    @pl.when(j == 0)
    def _():
        ub_ref[...] = uc_ref[...].astype(BF16)

    ub = ub_ref[:, pl.ds(pl.multiple_of(_gate_window(j), LANES), GATE_TK)]
    r = jax.nn.sigmoid(_dot(ub, wa_ref[...]) + ba_ref[...])
    i = jax.nn.sigmoid(_dot(ub, wx_ref[...]) + bx_ref[...])
    log_a = -RG_C * r * _softplus(-lam_ref[...])
    a = jnp.exp(log_a)
    u = uc_ref[:, pl.ds(pl.multiple_of(j * GATE_TN, LANES), GATE_TN)]
    a_ref[...] = a
    xin_ref[...] = jnp.sqrt(1.0 - a * a) * i * u


def _gates(uc, wa_b, wx_b, ba, bx, lam, layer):
    m = uc.shape[0]
    tm = _row_tile(m, 1024)
    vec = pl.BlockSpec((None, 1, GATE_TN), lambda i, j: (layer, 0, j))
    wsp = pl.BlockSpec((None, None, GATE_TK, GATE_TN), lambda i, j: (layer, j, 0, 0))
    osp = pl.BlockSpec((tm, GATE_TN), lambda i, j: (i, j))
    return pl.pallas_call(
        _gates_body,
        grid=(m // tm, D_RNN // GATE_TN),
        in_specs=[pl.BlockSpec((tm, D_RNN), lambda i, j: (i, 0)), wsp, wsp, vec, vec, vec],
        out_specs=[osp, osp],
        out_shape=[jax.ShapeDtypeStruct((m, D_RNN), F32), jax.ShapeDtypeStruct((m, D_RNN), F32)],
        scratch_shapes=[pltpu.VMEM((tm, D_RNN), BF16)],
        compiler_params=_params("arbitrary", "arbitrary"),
        name="gates",
    )(uc, wa_b, wx_b, ba, bx, lam)


def _gates_scan_body(uc_ref, wa_ref, wx_ref, ba_ref, bx_ref, lam_ref, gate_ref, h0_ref,
                     y_ref, ht_ref, ub_ref, a_ref, xin_ref, carry_ref, *, tiles_per_seq):
    i, j = pl.program_id(0), pl.program_id(1)

    @pl.when((i == 0) & (j == 0))
    def _():
        carry_ref[...] = jnp.zeros_like(carry_ref)

    _gates_body(uc_ref, wa_ref, wx_ref, ba_ref, bx_ref, lam_ref, a_ref, xin_ref, ub_ref)
    cols = pl.ds(pl.multiple_of(j * GATE_TN, LANES), GATE_TN)
    h = jnp.where(i % tiles_per_seq == 0, h0_ref[...], carry_ref[0:1, cols])
    h = _scan_rows(a_ref, xin_ref, gate_ref, y_ref, h, a_ref.shape[0])
    carry_ref[0:1, cols] = h
    ht_ref[...] = h


def _gates_scan(uc, gate, h0, wa_b, wx_b, ba, bx, lam, layer, t_len):
    m = uc.shape[0]
    tm = _row_tile(m, 1024)
    tiles_per_seq = t_len // tm
    assert tiles_per_seq * tm == t_len
    body = functools.partial(_gates_scan_body, tiles_per_seq=tiles_per_seq)
    vec = pl.BlockSpec((None, 1, GATE_TN), lambda i, j: (layer, 0, j))
    wsp = pl.BlockSpec((None, None, GATE_TK, GATE_TN), lambda i, j: (layer, j, 0, 0))
    tile = pl.BlockSpec((tm, GATE_TN), lambda i, j: (i, j))
    hsp = pl.BlockSpec((None, 1, GATE_TN), lambda i, j: (i // tiles_per_seq, 0, j))
    y, h_tiles = pl.pallas_call(
        body,
        grid=(m // tm, D_RNN // GATE_TN),
        in_specs=[pl.BlockSpec((tm, D_RNN), lambda i, j: (i, 0)), wsp, wsp, vec, vec, vec, tile, hsp],
        out_specs=[tile, pl.BlockSpec((None, 1, GATE_TN), lambda i, j: (i, 0, j))],
        out_shape=[jax.ShapeDtypeStruct((m, D_RNN), BF16),
                   jax.ShapeDtypeStruct((m // tm, 1, D_RNN), F32)],
        scratch_shapes=[pltpu.VMEM((tm, D_RNN), BF16), pltpu.VMEM((tm, GATE_TN), F32),
                        pltpu.VMEM((tm, GATE_TN), F32), pltpu.VMEM((SUBLANES, D_RNN), F32)],
        compiler_params=_params("arbitrary", "arbitrary"),
        name="gates_scan",
    )(uc, wa_b, wx_b, ba, bx, lam, gate, h0)
    return y, h_tiles[tiles_per_seq - 1::tiles_per_seq]


def _scan_rows(a_ref, x_ref, gate_ref, y_ref, h, t_len):
    tc = a_ref.shape[-1]
    step = 2 * SUBLANES
    if t_len % step == 0:
        row = lax.broadcasted_iota(jnp.int32, (SUBLANES, tc), 0)

        def chunk(sl, h):
            a = a_ref[sl, :]
            x = x_ref[sl, :]
            d = 1
            while d < SUBLANES:
                valid = row >= d
                x = jnp.where(valid, a * pltpu.roll(x, d, 0) + x, x)
                a = jnp.where(valid, a * pltpu.roll(a, d, 0), a)
                d *= 2
            return a * h + x

        def body(c, h):
            base = pl.multiple_of(c * step, step)
            hs1 = chunk(pl.ds(base, SUBLANES), h)
            hs2 = chunk(pl.ds(pl.multiple_of(base + SUBLANES, SUBLANES), SUBLANES),
                        hs1[SUBLANES - 1:SUBLANES, :])
            sl = pl.ds(base, step)
            y_ref[sl, :] = (jnp.concatenate([hs1, hs2], axis=0) * gate_ref[sl, :]).astype(y_ref.dtype)
            return hs2[SUBLANES - 1:SUBLANES, :]

        h = lax.fori_loop(0, t_len // step, body, h)
    else:
        for t in range(t_len):
            h = a_ref[t:t + 1, :] * h + x_ref[t:t + 1, :]
            y_ref[t:t + 1, :] = (h * gate_ref[t:t + 1, :]).astype(y_ref.dtype)
    return h


def _scan_body(a_ref, x_ref, gate_ref, h0_ref, y_ref, ht_ref, *, t_len):
    ht_ref[...] = _scan_rows(a_ref, x_ref, gate_ref, y_ref, h0_ref[...], t_len)


def _scan(a, xin, gate, h0):
    bsz, t_len, _ = a.shape
    tc = _chan_tile(t_len)
    body = functools.partial(_scan_body, t_len=t_len)
    seq_spec = pl.BlockSpec((None, t_len, tc), lambda b_, c: (b_, 0, c))
    h_spec = pl.BlockSpec((None, 1, tc), lambda b_, c: (b_, 0, c))
    y_dtype = BF16 if t_len % (2 * SUBLANES) == 0 else F32
    return pl.pallas_call(
        body,
        grid=(bsz, D_RNN // tc),
        in_specs=[seq_spec, seq_spec, seq_spec, h_spec],
        out_specs=[seq_spec, h_spec],
        out_shape=[jax.ShapeDtypeStruct(a.shape, y_dtype), jax.ShapeDtypeStruct((bsz, 1, D_RNN), F32)],
        compiler_params=_params("arbitrary", "arbitrary"),
        name="scan",
    )(a, xin, gate, h0)


def _rope_tables(pos):
    half = ROT_DIM // 2
    inv = ROPE_THETA ** (-jnp.arange(0, ROT_DIM, 2, dtype=F32) / ROT_DIM)
    ang = pos.astype(F32)[:, None] * inv[None, :]
    cos, sin = jnp.cos(ang), jnp.sin(ang)
    t_len = pos.shape[0]
    ones = jnp.ones((t_len, HEAD_DIM - ROT_DIM), F32)
    zer = jnp.zeros((t_len, HEAD_DIM - ROT_DIM), F32)
    zh = jnp.zeros((t_len, half), F32)
    c = jnp.concatenate([cos, cos, ones], axis=1)
    sa = jnp.concatenate([-sin, zh, zer], axis=1)
    sb = jnp.concatenate([zh, sin, zer], axis=1)
    return tuple(jnp.concatenate([t, t], axis=1) for t in (c, sa, sb))


def _trunk(x, bsz, t_len, p, ffn_w, tabs, period_rows, attn_fn, h0, buf0):
    hs, bufs, emitted = [], [], []
    n_attn = (DEPTH + 1) // 2
    ks = jnp.zeros((n_attn, x.shape[0], ATTN_WIDTH), F32)
    vs = jnp.zeros((n_attn, x.shape[0], ATTN_WIDTH), F32)

    def ffn(x, g, which, l, gf=None):
        if ffn_w is not None:
            return _ffn(x, g, *ffn_w[l][which], gf=gf), None
        names = ("pre_w1", "pre_w3", "pre_w2") if which == 0 else ("post_w1", "post_w3", "post_w2")
        y, *wb = _ffn(x, g, *(p[n] for n in names), gf=gf, layer=l)
        return y, tuple(wb)

    for l in range(DEPTH):
        j = l // 2
        x, wb_pre = ffn(x, p["ln_ffn_pre"][l:l + 1], 0, l)
        g_mix = p["ln_mix"][l:l + 1]
        if l % 2 == 0:
            lam_init = 0.8 - 0.6 * math.exp(-0.3 * l)
            q = _proj(x, g_mix, p["wq"], j, "q", tabs=tabs, period_rows=period_rows)
            ks = _proj(x, g_mix, p["wk"], j, "k", tabs=tabs, period_rows=period_rows,
                       stack=(n_attn, j, ks))
            vs = _proj(x, g_mix, p["wv"], j, "v", stack=(n_attn, j, vs))
            o = attn_fn(j, q, ks, vs, p["lp"][j], p["attn_subln"][j:j + 1], lam_init)
            x = _mm_res(o, p["wo"], j, x)
        else:
            gate = _proj(x, g_mix, p["w_gate"], j, "gelu", bias=p["b_gate"])
            if t_len % _proj_row_tile(bsz * t_len, D_RNN) == 0:
                uc, nb = _proj(x, g_mix, p["w_in"], j, "conv", bias=p["b_in"],
                               conv=(p["conv_w"], p["conv_b"], buf0[j], t_len))
            else:
                u = _proj(x, g_mix, p["w_in"], j, "bias", bias=p["b_in"])
                uc, nb = _conv(u.reshape(bsz, t_len, D_RNN), buf0[j], p["conv_w"], p["conv_b"], j)
            gate_w = (p["wa_b"], p["wx_b"], p["ba"], p["bx"], p["lam"], j)
            uc = uc.reshape(bsz * t_len, D_RNN)
            if t_len % _row_tile(bsz * t_len, 1024) == 0:
                yg, ht = _gates_scan(uc, gate, h0[j][:, None, :], *gate_w, t_len)
            else:
                a, xin = _gates(uc, *gate_w)
                yg, ht = _scan(a.reshape(bsz, t_len, D_RNN), xin.reshape(bsz, t_len, D_RNN),
                               gate.reshape(bsz, t_len, D_RNN), h0[j][:, None, :])
            x = _mm_res(yg.reshape(bsz * t_len, D_RNN), p["w_out"], j, x)
            hs.append(ht[:, 0, :])
            bufs.append(nb)
        gf = p["ln_final"] if l == DEPTH - 1 else None
        x, wb_post = ffn(x, p["ln_ffn_post"][l:l + 1], 1, l, gf)
        emitted.append((wb_pre, wb_post))
    return x, ks, vs, hs, bufs, emitted


def kernel(x_prompt, x_sample, cache_k, cache_v, state_h, state_conv, page_table, ln_ffn_pre, ffn_pre_w1, ffn_pre_w3, ffn_pre_w2, ln_mix, ln_ffn_post, ffn_post_w1, ffn_post_w3, ffn_post_w2, attn_wq, attn_wk, attn_wv, attn_lq1, attn_lk1, attn_lq2, attn_lk2, attn_subln, attn_wo, rec_w_gate, rec_b_gate, rec_w_in, rec_b_in, rec_conv_w, rec_conv_b, rec_wa, rec_ba, rec_wx, rec_bx, rec_lam, rec_w_out, ln_final):
    n_b = rec_lam.shape[0]
    bsz, seq = x_prompt.shape[0], x_prompt.shape[1]
    db, dec_seq = x_sample.shape[0], x_sample.shape[1]
    past = page_table.shape[1] * cache_k.shape[2]

    p = {
        "ln_ffn_pre": ln_ffn_pre, "ln_mix": ln_mix, "ln_ffn_post": ln_ffn_post,
        "ln_final": ln_final[None, :],
        "pre_w1": ffn_pre_w1, "pre_w3": ffn_pre_w3, "pre_w2": ffn_pre_w2,
        "post_w1": ffn_post_w1, "post_w3": ffn_post_w3, "post_w2": ffn_post_w2,
        "wq": attn_wq.astype(BF16), "wk": attn_wk.astype(BF16), "wv": attn_wv.astype(BF16),
        "lp": jnp.stack([attn_lq1, attn_lk1, attn_lq2, attn_lk2], axis=1).astype(F32),
        "attn_subln": attn_subln, "wo": attn_wo.astype(BF16),
        "w_gate": rec_w_gate.astype(BF16), "w_in": rec_w_in.astype(BF16),
        "b_gate": rec_b_gate[:, None, :], "b_in": rec_b_in[:, None, :],
        "conv_w": rec_conv_w, "conv_b": rec_conv_b[:, None, :],
        "wa_b": _banded(rec_wa), "wx_b": _banded(rec_wx),
        "ba": rec_ba[:, None, :], "bx": rec_bx[:, None, :], "lam": rec_lam[:, None, :],
        "w_out": rec_w_out.astype(BF16),
    }

    def sample_mix(j, q, k, v, lp, g, lam_init):
        q5 = q.reshape(db, dec_seq, N_HEADS, 2, HEAD_DIM)
        zero = jnp.zeros_like(q5[:, :, :, 0])
        q1 = jnp.concatenate([q5[:, :, :, 0], zero], axis=-1)
        q2 = jnp.concatenate([zero, q5[:, :, :, 1]], axis=-1)
        qall = jnp.stack([q1, q2], axis=2).reshape(db, dec_seq, 2, N_GROUPS, HEAD_GROUP, V_DIM)
        qall = qall.transpose(0, 3, 1, 2, 4, 5).reshape(db, dec_seq * 2 * N_HEADS, V_DIM)
        o = _sample_attention(qall, k[j].reshape(db, dec_seq, N_HEADS, V_DIM),
                              v[j].reshape(db, dec_seq, N_HEADS, V_DIM), cache_k, cache_v, j,
                              page_table, lp, g, lam_init)
        return o.reshape(db * dec_seq, ATTN_WIDTH)

    pos_s = past + jnp.tile(jnp.arange(dec_seq), db)
    y_s, k_s, v_s, h_s, c_s, ffn_w = _trunk(
        x_sample.reshape(db * dec_seq, D_MODEL), db, dec_seq, p, None,
        _rope_tables(pos_s), db * dec_seq, sample_mix, state_h, state_conv)

    def prompt_mix(j, q, k, v, lp, g, lam_init):
        return _prompt_attention(q, k, v, j, lp, g, bsz, seq, lam_init)

    h0_p = jnp.zeros((n_b, bsz, D_RNN), F32)
    buf0_p = jnp.zeros((n_b, bsz, CONV_W - 1, D_RNN), F32)
    y_p, k_p, v_p, h_p, c_p, _ = _trunk(
        x_prompt.reshape(bsz * seq, D_MODEL), bsz, seq, p, ffn_w,
        _rope_tables(jnp.arange(seq)), seq, prompt_mix, h0_p, buf0_p)

    def kv(xs, b_, t_):
        return xs.reshape(xs.shape[0], b_, t_, N_HEADS, V_DIM)

    return (y_p.reshape(bsz, seq, D_MODEL), y_s.reshape(db, dec_seq, D_MODEL),
            kv(k_p, bsz, seq), kv(v_p, bsz, seq), jnp.stack(h_p), jnp.stack(c_p),
            kv(k_s, db, dec_seq), kv(v_s, db, dec_seq), jnp.stack(h_s), jnp.stack(c_s))
```
